```python
import math
import jax, jax.numpy as jnp
from jax import lax
import numpy as np

D_MODEL = 1024
BATCH = 8
SEQ = 4096
DEPTH = 2

HEAD_DIM = 64
D_ATTN = D_MODEL // 2
N_HEADS = D_ATTN // HEAD_DIM
DILATED_GROUPS = ((128, 1), (512, 4), (2048, 16))
N_GROUPS = len(DILATED_GROUPS)
N_QKV = 3 * N_GROUPS * N_HEADS * HEAD_DIM
D_HYENA = D_MODEL // 2
HYENA_ORDER = 2
FILTER_EMB = 33
FILTER_WIDTH = 64
DECAY_TARGET = 1e-2
FAST_DECAY_PCT = 0.3
SLOW_DECAY_PCT = 1.5
N_IN = N_QKV + D_ATTN + 3 * D_HYENA + D_HYENA + 2 * D_MODEL
DEEPNORM_ALPHA = (2 * DEPTH) ** 0.25
DEEPNORM_BETA = (8 * DEPTH) ** -0.25
LN_EPS = 1e-5

kernel_name = "dilated_attn_hyena_hybrid_encoder"


def _layer_norm(x):
    xf = x.astype(jnp.float32)
    mu = jnp.mean(xf, axis=-1, keepdims=True)
    var = jnp.mean(jnp.square(xf - mu), axis=-1, keepdims=True)
    return (xf - mu) * lax.rsqrt(var + LN_EPS)


def _alibi_slopes(n_heads):
    return jnp.asarray(2.0 ** (-8.0 * (np.arange(n_heads) + 1) / n_heads), dtype=jnp.float32)


def dilated_window_attention(q, k, v, slopes, dilation, radius):
    B, S, H, E = q.shape
    L = S // dilation
    nb = -(-L // radius)
    Lp = nb * radius

    def by_residue(a):
        return a.reshape(B, L, dilation, H, E).transpose(0, 2, 1, 3, 4)

    qs, ks, vs = by_residue(q), by_residue(k), by_residue(v)
    qb = jnp.pad(qs, ((0, 0), (0, 0), (0, Lp - L), (0, 0), (0, 0))).reshape(B, dilation, nb, radius, H, E)

    def windows(a):
        ap = jnp.pad(a, ((0, 0), (0, 0), (radius, Lp - L + radius), (0, 0), (0, 0)))
        ap = ap.reshape(B, dilation, nb + 2, radius, H, E)
        return jnp.concatenate([ap[:, :, :-2], ap[:, :, 1:-1], ap[:, :, 2:]], axis=3)

    kw, vw = windows(ks), windows(vs)
    qi = jnp.arange(nb)[:, None] * radius + jnp.arange(radius)[None, :]
    ki = jnp.arange(nb)[:, None] * radius - radius + jnp.arange(3 * radius)[None, :]
    rel = ki[:, None, :] - qi[:, :, None]
    valid = (jnp.abs(rel) <= radius) & (ki[:, None, :] >= 0) & (ki[:, None, :] < L)
    dist = (jnp.abs(rel) * dilation).astype(jnp.float32)

    s = jnp.einsum('bdnqhe,bdnkhe->bdnhqk', qb, kw).astype(jnp.float32) * (E ** -0.5)
    s = s - slopes[:, None, None] * dist[:, None]
    s = jnp.where(valid[:, None], s, -jnp.inf)
    lse = jax.nn.logsumexp(s, axis=-1)
    p = jnp.exp(s - lse[..., None]).astype(v.dtype)
    o = jnp.einsum('bdnhqk,bdnkhe->bdnqhe', p, vw)

    o = o.reshape(B, dilation, Lp, H, E)[:, :, :L].transpose(0, 2, 1, 3, 4).reshape(B, S, H, E)
    lse = lse.transpose(0, 1, 2, 4, 3).reshape(B, dilation, Lp, H)[:, :, :L]
    lse = lse.transpose(0, 2, 1, 3).reshape(B, S, H)
    return o, lse


def _hyena_pos_features(L):
    t = jnp.linspace(0.0, 1.0, L, dtype=jnp.float32)[:, None]
    bands = (FILTER_EMB - 1) // 2
    w = 2.0 * math.pi * jnp.arange(L, dtype=jnp.float32)[:, None] / L
    f = jnp.linspace(1e-4, bands - 1, bands, dtype=jnp.float32)[None, :]
    return jnp.concatenate([t, jnp.cos(f * w), -jnp.sin(f * w)], axis=-1)


def implicit_filters(L, w1, b1, w2, b2, w3, b3, w4, freq):
    f32 = jnp.float32
    freq = freq.astype(f32)
    h = jnp.sin(freq * (_hyena_pos_features(L) @ w1.astype(f32) + b1.astype(f32)))
    h = jnp.sin(freq * (h @ w2.astype(f32) + b2.astype(f32)))
    h = jnp.sin(freq * (h @ w3.astype(f32) + b3.astype(f32)))
    h = h @ w4.astype(f32)
    t = jnp.linspace(0.0, 1.0, L, dtype=f32)[:, None]
    deltas = jnp.linspace(math.log(DECAY_TARGET) / SLOW_DECAY_PCT,
                          math.log(DECAY_TARGET) / FAST_DECAY_PCT, D_HYENA, dtype=f32)
    decay = jnp.exp(-t * jnp.abs(deltas)[None, :])
    return h.reshape(L, HYENA_ORDER, 2, D_HYENA) * decay[:, None, None, :]


def bidirectional_long_conv(z, h_fwd, h_bwd):
    L, C = h_fwd.shape
    circ = jnp.concatenate([h_fwd, jnp.zeros((1, C), jnp.float32), h_bwd[:0:-1]], axis=0)
    hf = jnp.fft.rfft(circ, axis=0)
    zf = jnp.fft.rfft(z.astype(jnp.float32), n=2 * L, axis=1)
    y = jnp.fft.irfft(zf * hf[None], n=2 * L, axis=1)[:, :L]
    return y.astype(z.dtype)


def centred_conv3(u, w, b):
    up = jnp.pad(u, ((0, 0), (1, 1), (0, 0)))
    return w[0] * up[:, :-2] + w[1] * up[:, 1:-1] + w[2] * up[:, 2:] + b


def hybrid_layer(x, c, w_ada, b_ada, w_in, b_in, conv_w, conv_b,
                 filt_w1, filt_b1, filt_w2, filt_b2, filt_w3, filt_b3, filt_w4, filt_freq, filt_bias,
                 w_proj_attn, w_proj_hyena, w_out, b_out, ln_g, ln_b):
    B, S, _ = x.shape
    mod = (c @ w_ada + b_ada)[:, None, :]
    shift, scale, gate = jnp.split(mod, 3, axis=-1)
    h = (_layer_norm(x) * (1.0 + scale) + shift).astype(x.dtype)

    p = h @ w_in + b_in
    i1 = N_QKV
    i2 = i1 + D_ATTN
    i3 = i2 + 3 * D_HYENA
    i4 = i3 + D_HYENA
    p_qkv, g_attn, p_hy, g_hy, r_merge = jnp.split(p, [i1, i2, i3, i4], axis=-1)
    r_attn, r_hy = jnp.split(r_merge, 2, axis=-1)

    qkv = p_qkv.reshape(B, S, 3, N_GROUPS, N_HEADS, HEAD_DIM)
    slopes = _alibi_slopes(N_HEADS)
    outs, lses = [], []
    for g, (window, dilation) in enumerate(DILATED_GROUPS):
        radius = window // (2 * dilation)
        o_g, lse_g = dilated_window_attention(qkv[:, :, 0, g], qkv[:, :, 1, g], qkv[:, :, 2, g],
                                              slopes, dilation, radius)
        outs.append(o_g)
        lses.append(lse_g)
    wts = jax.nn.softmax(jnp.stack(lses, axis=0), axis=0)
    o_attn = jnp.einsum('gbsh,gbshe->bshe', wts.astype(x.dtype), jnp.stack(outs, axis=0))
    y_attn = o_attn.reshape(B, S, D_ATTN) * jax.nn.silu(g_attn)

    u = centred_conv3(p_hy, conv_w, conv_b)
    v, x1, x2 = jnp.split(u, 3, axis=-1)
    filters = implicit_filters(S, filt_w1, filt_b1, filt_w2, filt_b2, filt_w3, filt_b3, filt_w4, filt_freq)
    z = v
    for o, gate_o in enumerate((x1, x2)):
        z = gate_o * (bidirectional_long_conv(z, filters[:, o, 0], filters[:, o, 1]) + filt_bias[o] * z)
    y_hy = z * jax.nn.silu(g_hy)

    merged = jax.nn.sigmoid(r_attn) * (y_attn @ w_proj_attn) + jax.nn.sigmoid(r_hy) * (y_hy @ w_proj_hyena)
    out = merged @ w_out + b_out

    res = DEEPNORM_ALPHA * x.astype(jnp.float32) + (gate * out).astype(jnp.float32)
    return (_layer_norm(res) * ln_g + ln_b).astype(x.dtype)


def setup_inputs(seed: int = 0) -> dict:
    key = jax.random.key(seed)
    ks = jax.random.split(key, 24)
    f32 = jnp.float32

    def nrm(k, shape, scale):
        return jax.random.normal(k, shape, f32) * scale

    return {
        "x": nrm(ks[0], (BATCH, SEQ, D_MODEL), 1.0),
        "c": nrm(ks[1], (BATCH, D_MODEL), 1.0),
        "w_ada": nrm(ks[2], (DEPTH, D_MODEL, 3 * D_MODEL), 0.5 * D_MODEL ** -0.5),
        "b_ada": nrm(ks[3], (DEPTH, 3 * D_MODEL), 0.01),
        "w_in": nrm(ks[4], (DEPTH, D_MODEL, N_IN), D_MODEL ** -0.5),
        "b_in": nrm(ks[5], (DEPTH, N_IN), 0.01),
        "conv_w": nrm(ks[6], (DEPTH, 3, 3 * D_HYENA), 3 ** -0.5),
        "conv_b": nrm(ks[7], (DEPTH, 3 * D_HYENA), 0.01),
        "filt_w1": nrm(ks[8], (DEPTH, FILTER_EMB, FILTER_WIDTH), FILTER_EMB ** -0.5),
        "filt_b1": nrm(ks[9], (DEPTH, FILTER_WIDTH), 0.1),
        "filt_w2": nrm(ks[10], (DEPTH, FILTER_WIDTH, FILTER_WIDTH), FILTER_WIDTH ** -0.5),
        "filt_b2": nrm(ks[11], (DEPTH, FILTER_WIDTH), 0.1),
        "filt_w3": nrm(ks[12], (DEPTH, FILTER_WIDTH, FILTER_WIDTH), FILTER_WIDTH ** -0.5),
        "filt_b3": nrm(ks[13], (DEPTH, FILTER_WIDTH), 0.1),
        "filt_w4": nrm(ks[14], (DEPTH, FILTER_WIDTH, HYENA_ORDER * 2 * D_HYENA), 0.1 * FILTER_WIDTH ** -0.5),
        "filt_freq": 1.0 + nrm(ks[15], (DEPTH, FILTER_WIDTH), 0.05),
        "filt_bias": nrm(ks[16], (DEPTH, HYENA_ORDER, D_HYENA), 0.1),
        "w_proj_attn": nrm(ks[17], (DEPTH, D_ATTN, D_MODEL), DEEPNORM_BETA * D_ATTN ** -0.5),
        "w_proj_hyena": nrm(ks[18], (DEPTH, D_HYENA, D_MODEL), DEEPNORM_BETA * D_HYENA ** -0.5),
        "w_out": nrm(ks[19], (DEPTH, D_MODEL, D_MODEL), DEEPNORM_BETA * D_MODEL ** -0.5),
        "b_out": nrm(ks[20], (DEPTH, D_MODEL), 0.01),
        "ln_g": 1.0 + nrm(ks[21], (DEPTH, D_MODEL), 0.05),
        "ln_b": nrm(ks[22], (DEPTH, D_MODEL), 0.01),
    }


def reference(x, c, w_ada, b_ada, w_in, b_in, conv_w, conv_b,
              filt_w1, filt_b1, filt_w2, filt_b2, filt_w3, filt_b3, filt_w4, filt_freq, filt_bias,
              w_proj_attn, w_proj_hyena, w_out, b_out, ln_g, ln_b):
    for l in range(DEPTH):
        x = hybrid_layer(x, c, w_ada[l], b_ada[l], w_in[l], b_in[l], conv_w[l], conv_b[l],
                         filt_w1[l], filt_b1[l], filt_w2[l], filt_b2[l], filt_w3[l], filt_b3[l],
                         filt_w4[l], filt_freq[l], filt_bias[l],
                         w_proj_attn[l], w_proj_hyena[l], w_out[l], b_out[l], ln_g[l], ln_b[l])
    return x
```

```python
import functools
import math

import numpy as np
import jax
import jax.numpy as jnp
from jax import lax
from jax.experimental import pallas as pl
from jax.experimental.pallas import tpu as pltpu

F32 = jnp.float32
BF16 = jnp.bfloat16

HEAD_DIM = 64
DILATED_GROUPS = ((128, 1), (512, 4), (2048, 16))
N_GROUPS = len(DILATED_GROUPS)
HYENA_ORDER = 2
FILTER_EMB = 33
FILTER_WIDTH = 64
DECAY_TARGET = 1e-2
FAST_DECAY_PCT = 0.3
SLOW_DECAY_PCT = 1.5
LN_EPS = 1e-5

LANES = 128
NB = 64
PAD_EMB = 128
NEG_BIG = -1e30

ROW_TILE = 1024
COL_TILE = 1536
OUT_TILE = 512
NB_STEP = 8
KA_STEP = 8


def _mm(a, b):
    return jnp.dot(a, b, preferred_element_type=F32)


def _split(a):
    hi = a.astype(BF16)
    lo = (a - hi.astype(F32)).astype(BF16)
    return hi, lo


def _dot3(a, b):
    ah, al = _split(a)
    bh, bl = _split(b)
    return _mm(ah, bh) + _mm(ah, bl) + _mm(al, bh)


def _layer_norm(x):
    mu = jnp.mean(x, axis=-1, keepdims=True)
    xc = x - mu
    var = jnp.mean(xc * xc, axis=-1, keepdims=True)
    return xc * lax.rsqrt(var + LN_EPS)


def _silu(a):
    return a * jax.nn.sigmoid(a)


@functools.lru_cache(maxsize=None)
def _dft_consts(seq):
    n = 2 * seq
    na_full = n // NB
    na_half = na_full // 2
    ka = np.arange(na_full)[:, None]
    na = np.arange(na_half)[None, :]
    ang = 2.0 * np.pi * ((ka * na) % na_full) / na_full
    c, s = np.cos(ang), np.sin(ang)
    f1 = np.block([[c, s], [-s, c]])
    f1r = np.concatenate([c, -s], axis=0)
    f4 = np.block([[c.T, -s.T], [s.T, c.T]]) / n
    kav = np.arange(na_full)[:, None, None]
    kb = np.arange(NB)[None, :, None]
    nb = np.arange(NB)[None, None, :]
    th = 2.0 * np.pi * (((nb * kav) % n) / n + ((nb * kb) % NB) / NB)
    cg, sg = np.cos(th), np.sin(th)
    g = np.concatenate([np.concatenate([cg, sg], axis=2), np.concatenate([-sg, cg], axis=2)], axis=1)
    cgt, sgt = cg.transpose(0, 2, 1), sg.transpose(0, 2, 1)
    gi = np.concatenate([np.concatenate([cgt, -sgt], axis=2), np.concatenate([sgt, cgt], axis=2)], axis=1)
    as_bf16 = lambda a: jnp.asarray(a, dtype=F32).astype(BF16)
    return as_bf16(f1), as_bf16(f1r), as_bf16(f4), as_bf16(g), as_bf16(gi)


@functools.lru_cache(maxsize=None)
def _filter_consts(seq, channels):
    t = np.linspace(0.0, 1.0, seq)[:, None]
    bands = (FILTER_EMB - 1) // 2
    w = 2.0 * np.pi * np.arange(seq)[:, None] / seq
    f = np.linspace(1e-4, bands - 1, bands)[None, :]
    pos = np.concatenate([t, np.cos(f * w), -np.sin(f * w)], axis=-1)
    pos = np.pad(pos, ((0, 0), (0, PAD_EMB - FILTER_EMB)))
    deltas = np.linspace(math.log(DECAY_TARGET) / SLOW_DECAY_PCT, math.log(DECAY_TARGET) / FAST_DECAY_PCT, channels)
    decay = np.exp(-t * np.abs(deltas)[None, :])
    return jnp.asarray(pos, dtype=F32), jnp.asarray(decay, dtype=F32)


@functools.lru_cache(maxsize=None)
def _attn_bias(n_heads, dilation, radius):
    tq, tk = 2 * radius, 4 * radius
    slopes = 2.0 ** (-8.0 * (np.arange(n_heads) + 1) / n_heads)
    i = np.arange(tq)[:, None]
    j = np.arange(tk)[None, :]
    cases = []
    for off in (0, radius, 2 * radius):
        rel = np.abs(j - off - i)
        per_head = [np.where(rel <= radius, -slopes[h] * rel * dilation, NEG_BIG) for h in range(n_heads)]
        cases.append(np.stack([np.concatenate(per_head[2 * p:2 * p + 2], axis=0) for p in range(n_heads // 2)]))
    return jnp.asarray(np.stack(cases), dtype=F32)


@functools.lru_cache(maxsize=None)
def _head_expand(n_heads):
    e = np.zeros((LANES, n_heads * HEAD_DIM))
    for h in range(n_heads):
        e[h, h * HEAD_DIM:(h + 1) * HEAD_DIM] = 1.0
    return jnp.asarray(e, dtype=F32).astype(BF16)


def _ada_kernel(c_ref, w_ref, b_ref, o_ref):
    o_ref[...] = _dot3(c_ref[...], w_ref[...]) + b_ref[...]


def _ada(c, w_ada, b_ada):
    depth, d, n3 = w_ada.shape
    bsz = c.shape[0]
    tn = d
    return pl.pallas_call(
        _ada_kernel,
        grid=(depth, n3 // tn),
        in_specs=[
            pl.BlockSpec((bsz, d), lambda l, j: (0, 0)),
            pl.BlockSpec((None, d, tn), lambda l, j: (l, 0, j)),
            pl.BlockSpec((None, 1, tn), lambda l, j: (l, 0, j)),
        ],
        out_specs=pl.BlockSpec((None, bsz, tn), lambda l, j: (l, 0, j)),
        out_shape=jax.ShapeDtypeStruct((depth, bsz, n3), F32),
        name="ada_mod",
    )(c, w_ada, b_ada.reshape(depth, 1, n3))


def _inproj_kernel(x_ref, mod_ref, w_ref, b_ref, o_ref, h_ref):
    @pl.when(pl.program_id(2) == 0)
    def _():
        hn = _layer_norm(x_ref[...])
        shift, scale = mod_ref[0:1, :], mod_ref[1:2, :]
        h_ref[...] = (hn * (1.0 + scale) + shift).astype(BF16)

    o_ref[...] = (_mm(h_ref[...], w_ref[...]) + b_ref[...]).astype(o_ref.dtype)


def _inproj(x, mod, w, b):
    bsz, seq, d = x.shape
    n = w.shape[1]
    tm = ROW_TILE
    return pl.pallas_call(
        _inproj_kernel,
        grid=(bsz, seq // tm, n // COL_TILE),
        in_specs=[
            pl.BlockSpec((None, tm, d), lambda bi, i, j: (bi, i, 0)),
            pl.BlockSpec((None, 3, d), lambda bi, i, j: (bi, 0, 0)),
            pl.BlockSpec((d, COL_TILE), lambda bi, i, j: (0, j)),
            pl.BlockSpec((1, COL_TILE), lambda bi, i, j: (0, j)),
        ],
        out_specs=pl.BlockSpec((None, tm, COL_TILE), lambda bi, i, j: (bi, i, j)),
        out_shape=jax.ShapeDtypeStruct((bsz, seq, n), BF16),
        scratch_shapes=[pltpu.VMEM((tm, d), BF16)],
        compiler_params=pltpu.CompilerParams(dimension_semantics=("parallel", "parallel", "arbitrary")),
        name="inproj",
    )(x, mod, w, b)


def _attn_kernel(q_ref, k_ref, v_ref, bias_ref, o_ref, lse_ref, *, length, radius, n_pairs):
    tq, tk = 2 * radius, 4 * radius
    rows = q_ref.shape[0]
    base = pl.program_id(2) * rows
    lane = lax.broadcasted_iota(jnp.int32, (tq, LANES), 1)
    first_head = lane < HEAD_DIM

    def block(jb, carry):
        q0l = pl.multiple_of(jb * tq, tq)
        q0 = base + q0l
        ks = pl.multiple_of(jnp.clip(q0 - radius, 0, length - tk), radius)
        case = jnp.where(q0 == 0, 0, jnp.where(q0 == length - tq, 2, 1))
        lse_tile = jnp.zeros((tq, LANES), F32)
        for p in range(n_pairs):
            cols = slice(p * LANES, (p + 1) * LANES)
            q2 = q_ref[pl.ds(q0l, tq), cols] * (HEAD_DIM ** -0.5)
            k2 = k_ref[pl.ds(ks, tk), cols]
            v2 = v_ref[pl.ds(ks, tk), cols]
            zero = jnp.zeros_like(q2)
            qs = jnp.concatenate([jnp.where(first_head, q2, zero), jnp.where(first_head, zero, q2)], axis=0)
            s = lax.dot_general(qs, k2, (((1,), (1,)), ((), ())), preferred_element_type=F32)
            s = s + bias_ref[case, p]
            m = jnp.max(s, axis=1, keepdims=True)
            e = jnp.exp(s - m)
            l = jnp.sum(e, axis=1, keepdims=True)
            eb = e.astype(BF16)
            oa = _mm(eb[:tq], v2) / l[:tq]
            ob = _mm(eb[tq:], v2) / l[tq:]
            o_ref[pl.ds(q0l, tq), cols] = jnp.where(first_head, oa, ob).astype(o_ref.dtype)
            lse = m + jnp.log(l)
            lse_tile = jnp.where(lane == 2 * p, lse[:tq], lse_tile)
            lse_tile = jnp.where(lane == 2 * p + 1, lse[tq:], lse_tile)
        lse_ref[pl.ds(q0l, tq), :] = lse_tile
        return carry

    lax.fori_loop(0, rows // tq, block, 0)


def _attention(qkv, col_block0, dilation, radius, n_heads):
    bsz, d, length, _ = qkv.shape
    da = n_heads * HEAD_DIM
    rows = min(ROW_TILE, length)
    bias = _attn_bias(n_heads, dilation, radius)
    kern = functools.partial(_attn_kernel, length=length, radius=radius, n_pairs=n_heads // 2)
    return pl.pallas_call(
        kern,
        grid=(bsz, d, length // rows),
        in_specs=[
            pl.BlockSpec((None, None, rows, da), lambda bi, r, i: (bi, r, i, col_block0)),
            pl.BlockSpec((None, None, length, da), lambda bi, r, i: (bi, r, 0, col_block0 + 1)),
            pl.BlockSpec((None, None, length, da), lambda bi, r, i: (bi, r, 0, col_block0 + 2)),
            pl.BlockSpec(bias.shape, lambda bi, r, i: (0, 0, 0, 0)),
        ],
        out_specs=[
            pl.BlockSpec((None, None, rows, da), lambda bi, r, i: (bi, r, i, 0)),
            pl.BlockSpec((None, None, rows, LANES), lambda bi, r, i: (bi, r, i, 0)),
        ],
        out_shape=[
            jax.ShapeDtypeStruct((bsz, d, length, da), BF16),
            jax.ShapeDtypeStruct((bsz, d, length, LANES), F32),
        ],
        name=f"attn_d{dilation}",
    )(qkv, qkv, qkv, bias)


def _to_residue_major(a, dilation):
    bsz, seq, n = a.shape
    return a.reshape(bsz, seq // dilation, dilation, n).transpose(0, 2, 1, 3)


def _to_natural(a):
    bsz, d, length, n = a.shape
    return a.transpose(0, 2, 1, 3).reshape(bsz, d * length, n)


def _filter_kernel(pos_ref, w1_ref, b1_ref, w2_ref, b2_ref, w3_ref, b3_ref, w4_ref, fr_ref, dec_ref, o_ref):
    fr = fr_ref[...]
    h = jnp.sin(fr * (_dot3(pos_ref[...], w1_ref[...]) + b1_ref[...]))
    h = jnp.sin(fr * (_dot3(h, w2_ref[...]) + b2_ref[...]))
    h = jnp.sin(fr * (_dot3(h, w3_ref[...]) + b3_ref[...]))
    f = _dot3(h, w4_ref[...])
    dec = dec_ref[...]
    c = dec.shape[1]
    row = pl.program_id(0) * pos_ref.shape[0] + lax.broadcasted_iota(jnp.int32, (pos_ref.shape[0], 1), 0)
    for o in range(HYENA_ORDER):
        fwd = f[:, (2 * o) * c:(2 * o + 1) * c] * dec
        bwd = jnp.where(row == 0, 0.0, f[:, (2 * o + 1) * c:(2 * o + 2) * c] * dec)
        o_ref[:, (2 * o) * c:(2 * o + 1) * c] = fwd + bwd
        o_ref[:, (2 * o + 1) * c:(2 * o + 2) * c] = bwd - fwd


def _filters(seq, c, w1, b1, w2, b2, w3, b3, w4, freq):
    pos, decay = _filter_consts(seq, c)
    tm = 512
    w1p = jnp.pad(w1, ((0, PAD_EMB - FILTER_EMB), (0, 0)))
    full = lambda a: pl.BlockSpec(a.shape, lambda i: (0,) * a.ndim)
    args = (w1p, b1.reshape(1, -1), w2, b2.reshape(1, -1), w3, b3.reshape(1, -1), w4, freq.reshape(1, -1))
    return pl.pallas_call(
        _filter_kernel,
        grid=(seq // tm,),
        in_specs=[pl.BlockSpec((tm, PAD_EMB), lambda i: (i, 0))] + [full(a) for a in args]
        + [pl.BlockSpec((tm, c), lambda i: (i, 0))],
        out_specs=pl.BlockSpec((tm, 2 * HYENA_ORDER * c), lambda i: (i, 0)),
        out_shape=jax.ShapeDtypeStruct((seq, 2 * HYENA_ORDER * c), F32),
        name="hyena_filters",
    )(pos, *args, decay)


def _store_spectrum(y_ref, l, y):
    half, c = y.shape[0] // 2, y.shape[1]
    y_ref[l, :, 0:c] = y[:half].astype(y_ref.dtype)
    y_ref[l, :, c:2 * c] = y[half:].astype(y_ref.dtype)


def _load_spectrum(blk):
    c = blk.shape[1] // 2
    return jnp.concatenate([blk[:, :c], blk[:, c:]], axis=0)


def _fstage1_kernel(f_ref, f1r_ref, y_ref):
    for l in range(f_ref.shape[0]):
        _store_spectrum(y_ref, l, _mm(f1r_ref[...], f_ref[l].astype(BF16)))


def _fstage2_kernel(e_ref, d_ref, g_ref, h_ref):
    for l in range(e_ref.shape[0]):
        h_ref[l, 0:NB, :] = _mm(g_ref[l, 0:NB, :], _load_spectrum(e_ref[l])).astype(h_ref.dtype)
        h_ref[l, NB:2 * NB, :] = (-_mm(g_ref[l, NB:2 * NB, :], _load_spectrum(d_ref[l]))).astype(h_ref.dtype)


def _filter_spectra(filt, seq, c):
    _, f1r, _, g, _ = _dft_consts(seq)
    na_full = 2 * seq // NB
    na_half = na_full // 2
    n_cb = 2 * HYENA_ORDER
    fv = filt.reshape(na_half, NB, n_cb * c).transpose(1, 0, 2)
    yf = pl.pallas_call(
        _fstage1_kernel,
        grid=(n_cb, NB // NB_STEP),
        in_specs=[pl.BlockSpec((NB_STEP, na_half, c), lambda cb, s: (s, 0, cb)),
                  pl.BlockSpec(f1r.shape, lambda cb, s: (0, 0))],
        out_specs=pl.BlockSpec((None, NB_STEP, na_full, 2 * c), lambda cb, s: (cb, s, 0, 0)),
        out_shape=jax.ShapeDtypeStruct((n_cb, NB, na_full, 2 * c), BF16),
        name="hyena_filter_stage1",
    )(fv, f1r)
    yf = yf.transpose(0, 2, 1, 3)
    return pl.pallas_call(
        _fstage2_kernel,
        grid=(HYENA_ORDER, na_full // KA_STEP),
        in_specs=[pl.BlockSpec((None, KA_STEP, NB, 2 * c), lambda o, s: (2 * o, s, 0, 0)),
                  pl.BlockSpec((None, KA_STEP, NB, 2 * c), lambda o, s: (2 * o + 1, s, 0, 0)),
                  pl.BlockSpec((KA_STEP, 2 * NB, 2 * NB), lambda o, s: (s, 0, 0))],
        out_specs=pl.BlockSpec((None, KA_STEP, 2 * NB, c), lambda o, s: (o, s, 0, 0)),
        out_shape=jax.ShapeDtypeStruct((HYENA_ORDER, na_full, 2 * NB, c), BF16),
        name="hyena_filter_stage2",
    )(yf, yf, g)


def _conv_stage1_kernel(p_ref, prev_ref, next_ref, g_ref, cw_ref, cb_ref, f1_ref, y_ref, u_ref):
    step, n_steps = pl.program_id(1), pl.num_programs(1)
    n_res, pair_rows, width = p_ref.shape
    c = width // 3
    half = pair_rows // 2
    row = lax.broadcasted_iota(jnp.int32, (pair_rows, 1), 0)
    first_na = (row == 0) | (row == half)
    last_na = (row == half - 1) | (row == pair_rows - 1)

    def residue(t):
        if t == 0:
            a = prev_ref[...].astype(F32)
            shifted = jnp.where(first_na, 0.0, pltpu.roll(a, 1, 0))
            return jnp.where(step == 0, shifted, a)
        if t == n_res + 1:
            a = next_ref[...].astype(F32)
            shifted = jnp.where(last_na, 0.0, pltpu.roll(a, pair_rows - 1, 0))
            return jnp.where(step == n_steps - 1, shifted, a)
        return p_ref[t - 1].astype(F32)

    w0, w1, w2, cb = cw_ref[0:1, :], cw_ref[1:2, :], cw_ref[2:3, :], cb_ref[...]
    for l in range(n_res):
        u = (w0 * residue(l) + w1 * residue(l + 1) + w2 * residue(l + 2) + cb).astype(BF16)
        u_ref[l, :, 0:width] = u
        u_ref[l, :, width:width + c] = _silu(g_ref[l].astype(F32)).astype(BF16)
        _store_spectrum(y_ref, l, _mm(f1_ref[...], u[:, 0:c]))


def _spectral_kernel(y_ref, hf_ref, g_ref, gi_ref, t_ref):
    for l in range(y_ref.shape[0]):
        x = _mm(g_ref[l], _load_spectrum(y_ref[l]))
        xr, xi = x[:NB], x[NB:]
        hr, hi = hf_ref[l, 0:NB, :].astype(F32), hf_ref[l, NB:2 * NB, :].astype(F32)
        prod = jnp.concatenate([xr * hr - xi * hi, xr * hi + xi * hr], axis=0).astype(BF16)
        _store_spectrum(t_ref, l, _mm(gi_ref[l], prod))


def _conv_mid_kernel(t_ref, v_ref, x1_ref, fb_ref, f4_ref, f1_ref, z_ref, y_ref):
    for l in range(t_ref.shape[0]):
        y = _mm(f4_ref[...], _load_spectrum(t_ref[l]))
        z = (x1_ref[l].astype(F32) * (y + fb_ref[...] * v_ref[l].astype(F32))).astype(BF16)
        z_ref[l] = z
        _store_spectrum(y_ref, l, _mm(f1_ref[...], z))


def _conv_last_kernel(t_ref, x2_ref, sg_ref, z1_ref, fb_ref, f4_ref, o_ref):
    for l in range(t_ref.shape[0]):
        y = _mm(f4_ref[...], _load_spectrum(t_ref[l]))
        z = x2_ref[l].astype(F32) * (y + fb_ref[...] * z1_ref[l].astype(F32))
        o_ref[l] = (z * sg_ref[l].astype(F32)).astype(o_ref.dtype)


def _hyena(p_hy, conv_w, conv_b, filt_bias, hf, c):
    bsz, seq, ncols = p_hy.shape
    f1, _, f4, g, gi = _dft_consts(seq)
    na_full = 2 * seq // NB
    na_half = na_full // 2
    n_pairs, pair_rows = bsz // 2, 2 * na_half
    width = 3 * c
    n_steps = NB // NB_STEP
    pv = p_hy.reshape(n_pairs, 2, na_half, NB, ncols).transpose(0, 3, 1, 2, 4).reshape(n_pairs, NB, pair_rows, ncols)
    const = lambda a: pl.BlockSpec(a.shape, lambda bp, s: (0,) * a.ndim)
    res_blk = lambda w, k=0: pl.BlockSpec((None, NB_STEP, pair_rows, w), lambda bp, s: (bp, s, 0, k))
    freq_blk = lambda rows: pl.BlockSpec((None, KA_STEP, rows, 2 * c), lambda s, bp: (bp, s, 0, 0))

    y_shape = jax.ShapeDtypeStruct((n_pairs, NB, na_full, 2 * c), BF16)
    y0, u = pl.pallas_call(
        _conv_stage1_kernel,
        grid=(n_pairs, n_steps),
        in_specs=[res_blk(width),
                  pl.BlockSpec((None, None, pair_rows, width), lambda bp, s: (bp, (s * NB_STEP + NB - 1) % NB, 0, 0)),
                  pl.BlockSpec((None, None, pair_rows, width), lambda bp, s: (bp, (s * NB_STEP + NB_STEP) % NB, 0, 0)),
                  res_blk(c, width // c), const(conv_w), pl.BlockSpec((1, width), lambda bp, s: (0, 0)), const(f1)],
        out_specs=[res_blk(2 * c), res_blk(width + c)],
        out_shape=[y_shape, jax.ShapeDtypeStruct((n_pairs, NB, pair_rows, width + c), BF16)],
        name="hyena_conv_stage1",
    )(pv, pv, pv, pv, conv_w, conv_b.reshape(1, width), f1)

    def spectral(y, order):
        t = pl.pallas_call(
            _spectral_kernel,
            grid=(na_full // KA_STEP, n_pairs),
            in_specs=[freq_blk(NB),
                      pl.BlockSpec((None, KA_STEP, 2 * NB, c), lambda s, bp: (order, s, 0, 0)),
                      pl.BlockSpec((KA_STEP, 2 * NB, 2 * NB), lambda s, bp: (s, 0, 0)),
                      pl.BlockSpec((KA_STEP, 2 * NB, 2 * NB), lambda s, bp: (s, 0, 0))],
            out_specs=freq_blk(NB),
            out_shape=jax.ShapeDtypeStruct((n_pairs, na_full, NB, 2 * c), BF16),
            name=f"hyena_spectral_o{order}",
        )(y.transpose(0, 2, 1, 3), hf, g, gi)
        return t.transpose(0, 2, 1, 3)

    t0 = spectral(y0, 0)
    fb = filt_bias.reshape(HYENA_ORDER, 1, c)
    z1, y1 = pl.pallas_call(
        _conv_mid_kernel,
        grid=(n_pairs, n_steps),
        in_specs=[res_blk(2 * c), res_blk(c, 0), res_blk(c, 1),
                  pl.BlockSpec((None, 1, c), lambda bp, s: (0, 0, 0)), const(f4), const(f1)],
        out_specs=[res_blk(c), res_blk(2 * c)],
        out_shape=[jax.ShapeDtypeStruct((n_pairs, NB, pair_rows, c), BF16), y_shape],
        name="hyena_conv_mid",
    )(t0, u, u, fb, f4, f1)

    t1 = spectral(y1, 1)
    out = pl.pallas_call(
        _conv_last_kernel,
        grid=(n_pairs, n_steps),
        in_specs=[res_blk(2 * c), res_blk(c, 2), res_blk(c, 3), res_blk(c),
                  pl.BlockSpec((None, 1, c), lambda bp, s: (1, 0, 0)), const(f4)],
        out_specs=res_blk(c),
        out_shape=jax.ShapeDtypeStruct((n_pairs, NB, pair_rows, c), BF16),
        name="hyena_conv_last",
    )(t1, u, u, z1, fb, f4)
    return out.reshape(n_pairs, NB, 2, na_half, c).transpose(0, 2, 3, 1, 4).reshape(bsz, seq, c)


def _out_kernel(o0_ref, o1_ref, o2_ref, l0_ref, l1_ref, l2_ref, ga_ref, yh_ref, ra_ref, rh_ref, x_ref, mod_ref,
                e_ref, wpa_ref, wph_ref, wo_ref, bo_ref, lg_ref, lb_ref, out_ref, *, alpha):
    lses = [l0_ref[...], l1_ref[...], l2_ref[...]]
    m = jnp.maximum(jnp.maximum(lses[0], lses[1]), lses[2])
    es = [jnp.exp(l - m) for l in lses]
    den = es[0] + es[1] + es[2]
    o = None
    for e, o_ref in zip(es, (o0_ref, o1_ref, o2_ref)):
        hi, lo = _split(e / den)
        w = _mm(hi, e_ref[...]) + _mm(lo, e_ref[...])
        term = w * o_ref[...].astype(F32)
        o = term if o is None else o + term
    ya = (o * _silu(ga_ref[...].astype(F32))).astype(BF16)
    merged = (jax.nn.sigmoid(ra_ref[...].astype(F32)) * _mm(ya, wpa_ref[...])
              + jax.nn.sigmoid(rh_ref[...].astype(F32)) * _mm(yh_ref[...], wph_ref[...]))
    out = _mm(merged.astype(BF16), wo_ref[...]) + bo_ref[...]
    res = alpha * x_ref[...] + mod_ref[2:3, :] * out
    out_ref[...] = _layer_norm(res) * lg_ref[...] + lb_ref[...]


def _out_proj(os_, lses, p, cols, yh, x, mod, wpa, wph, wo, bo, lg, lb, alpha):
    bsz, seq, d = x.shape
    da = wpa.shape[0]
    tm = OUT_TILE
    e = _head_expand(da // HEAD_DIM)
    rows = lambda width, cb=0: pl.BlockSpec((None, tm, width), lambda bi, i: (bi, i, cb))
    const = lambda a: pl.BlockSpec(a.shape, lambda bi, i: (0,) * a.ndim)
    ga_col, ra_col, rh_col = cols
    return pl.pallas_call(
        functools.partial(_out_kernel, alpha=alpha),
        grid=(bsz, seq // tm),
        in_specs=[rows(da)] * 3 + [rows(LANES)] * 3
        + [rows(da, ga_col // da), rows(da), rows(d, ra_col // d), rows(d, rh_col // d), rows(d),
           pl.BlockSpec((None, 3, d), lambda bi, i: (bi, 0, 0)),
           const(e), const(wpa), const(wph), const(wo)]
        + [pl.BlockSpec((1, d), lambda bi, i: (0, 0))] * 3,
        out_specs=rows(d),
        out_shape=jax.ShapeDtypeStruct((bsz, seq, d), F32),
        name="merge_out_proj",
    )(*os_, *lses, p, yh, p, p, x, mod, e, wpa, wph, wo, bo.reshape(1, d), lg.reshape(1, d), lb.reshape(1, d))


def _layer(x, mod, w_in, b_in, conv_w, conv_b, fw1, fb1, fw2, fb2, fw3, fb3, fw4, ffreq, fbias,
           w_proj_attn, w_proj_hyena, w_out, b_out, ln_g, ln_b, alpha):
    bsz, seq, d = x.shape
    da = w_proj_attn.shape[0]
    c = w_proj_hyena.shape[0]
    n_heads = da // HEAD_DIM

    def qkv_cols(a, g):
        return [a[..., (i * N_GROUPS + g) * da:(i * N_GROUPS + g + 1) * da] for i in range(3)]

    i1 = 3 * N_GROUPS * da
    i2 = i1 + da
    i4 = i2 + 4 * c

    def regroup(a):
        parts = [a[..., i2:i4]] + qkv_cols(a, 0) + [a[..., i1:i2], a[..., i4:]]
        for g in range(1, N_GROUPS):
            parts += qkv_cols(a, g)
        return jnp.concatenate(parts, axis=-1)

    col_qkv0 = 4 * c
    col_ga = col_qkv0 + 3 * da
    col_ra = col_ga + da
    col_rh = col_ra + d
    col_qkv = [col_qkv0] + [col_rh + d + (g - 1) * 3 * da for g in range(1, N_GROUPS)]

    p = _inproj(x, mod, regroup(w_in).astype(BF16), regroup(b_in).reshape(1, -1))

    outs, lses = [], []
    for g, (window, dilation) in enumerate(DILATED_GROUPS):
        radius = window // (2 * dilation)
        if dilation == 1:
            o_g, lse_g = _attention(p.reshape(bsz, 1, seq, p.shape[-1]), col_qkv[g] // da, 1, radius, n_heads)
        else:
            qkv = _to_residue_major(p[..., col_qkv[g]:col_qkv[g] + 3 * da], dilation)
            o_g, lse_g = _attention(qkv, 0, dilation, radius, n_heads)
        outs.append(_to_natural(o_g))
        lses.append(_to_natural(lse_g))

    filt = _filters(seq, c, fw1, fb1, fw2, fb2, fw3, fb3, fw4, ffreq)
    hf = _filter_spectra(filt, seq, c)
    yh = _hyena(p[..., :4 * c], conv_w, conv_b, fbias, hf, c)

    return _out_proj(outs, lses, p, (col_ga, col_ra, col_rh), yh, x, mod,
                     w_proj_attn.astype(BF16), w_proj_hyena.astype(BF16), w_out.astype(BF16),
                     b_out, ln_g, ln_b, alpha)


def kernel(x, c, w_ada, b_ada, w_in, b_in, conv_w, conv_b, filt_w1, filt_b1, filt_w2, filt_b2, filt_w3, filt_b3,
           filt_w4, filt_freq, filt_bias, w_proj_attn, w_proj_hyena, w_out, b_out, ln_g, ln_b):
    depth = w_ada.shape[0]
    alpha = (2 * depth) ** 0.25
    d = x.shape[-1]
    mods = _ada(c, w_ada, b_ada).reshape(depth, c.shape[0], 3, d)
    for l in range(depth):
        x = _layer(x, mods[l], w_in[l], b_in[l], conv_w[l], conv_b[l], filt_w1[l], filt_b1[l], filt_w2[l],
                   filt_b2[l], filt_w3[l], filt_b3[l], filt_w4[l], filt_freq[l], filt_bias[l],
                   w_proj_attn[l], w_proj_hyena[l], w_out[l], b_out[l], ln_g[l], ln_b[l], alpha)
    return x
```

```python
import functools
import math

import numpy as np
import jax
import jax.numpy as jnp
from jax import lax
from jax.experimental import pallas as pl
from jax.experimental.pallas import tpu as pltpu

F32 = jnp.float32
BF16 = jnp.bfloat16

HEAD_DIM = 64
DILATED_GROUPS = ((128, 1), (512, 4), (2048, 16))
N_GROUPS = len(DILATED_GROUPS)
HYENA_ORDER = 2
FILTER_EMB = 33
FILTER_WIDTH = 64
DECAY_TARGET = 1e-2
FAST_DECAY_PCT = 0.3
SLOW_DECAY_PCT = 1.5
LN_EPS = 1e-5

LANES = 128
NB = 64
PAD_EMB = 128
NEG_BIG = -1e30

ROW_TILE = 1024
COL_TILE = 1536
OUT_TILE = 512
NB_STEP = 8
KA_STEP = 8
HY_LANES = 256
HY_ROWS = 1024
HY_KA = 16
SUBLANES = 8
PITCH_T = NB + SUBLANES
PITCH_F = 2 * NB + SUBLANES


def _mm(a, b):
    return jnp.dot(a, b, preferred_element_type=F32)


def _split(a):
    hi = a.astype(BF16)
    lo = (a - hi.astype(F32)).astype(BF16)
    return hi, lo


def _dot3(a, b):
    ah, al = _split(a)
    bh, bl = _split(b)
    return _mm(ah, bh) + _mm(ah, bl) + _mm(al, bh)


def _layer_norm(x):
    mu = jnp.mean(x, axis=-1, keepdims=True)
    xc = x - mu
    var = jnp.mean(xc * xc, axis=-1, keepdims=True)
    return xc * lax.rsqrt(var + LN_EPS)


def _silu(a):
    return a * jax.nn.sigmoid(a)


@functools.lru_cache(maxsize=None)
def _dft_consts(seq):
    n = 2 * seq
    na_full = n // NB
    na_half = na_full // 2
    ka = np.arange(na_full)[:, None]
    na = np.arange(na_half)[None, :]
    ang = 2.0 * np.pi * ((ka * na) % na_full) / na_full
    c, s = np.cos(ang), np.sin(ang)
    f1 = np.block([[c, s], [-s, c]])
    f1r = np.concatenate([c, -s], axis=0)
    f4 = np.block([[c.T, -s.T], [s.T, c.T]]) / n
    kav = np.arange(na_full)[:, None, None]
    kb = np.arange(NB)[None, :, None]
    nb = np.arange(NB)[None, None, :]
    th = 2.0 * np.pi * (((nb * kav) % n) / n + ((nb * kb) % NB) / NB)
    cg, sg = np.cos(th), np.sin(th)
    g = np.concatenate([np.concatenate([cg, sg], axis=2), np.concatenate([-sg, cg], axis=2)], axis=1)
    cgt, sgt = cg.transpose(0, 2, 1), sg.transpose(0, 2, 1)
    gi = np.concatenate([np.concatenate([cgt, -sgt], axis=2), np.concatenate([sgt, cgt], axis=2)], axis=1)
    as_bf16 = lambda a: jnp.asarray(a, dtype=F32).astype(BF16)
    return as_bf16(f1), as_bf16(f1r), as_bf16(f4), as_bf16(g), as_bf16(gi)


@functools.lru_cache(maxsize=None)
def _filter_consts(seq, channels):
    t = np.linspace(0.0, 1.0, seq)[:, None]
    bands = (FILTER_EMB - 1) // 2
    w = 2.0 * np.pi * np.arange(seq)[:, None] / seq
    f = np.linspace(1e-4, bands - 1, bands)[None, :]
    pos = np.concatenate([t, np.cos(f * w), -np.sin(f * w)], axis=-1)
    pos = np.pad(pos, ((0, 0), (0, PAD_EMB - FILTER_EMB)))
    deltas = np.linspace(math.log(DECAY_TARGET) / SLOW_DECAY_PCT, math.log(DECAY_TARGET) / FAST_DECAY_PCT, channels)
    decay = np.exp(-t * np.abs(deltas)[None, :])
    return jnp.asarray(pos, dtype=F32), jnp.asarray(decay, dtype=F32)


@functools.lru_cache(maxsize=None)
def _attn_bias(n_heads, dilation, radius):
    tq, tk = 2 * radius, 4 * radius
    slopes = 2.0 ** (-8.0 * (np.arange(n_heads) + 1) / n_heads)
    i = np.arange(tq)[:, None]
    j = np.arange(tk)[None, :]
    cases = []
    for off in (0, radius, 2 * radius):
        rel = np.abs(j - off - i)
        per_head = [np.where(rel <= radius, -slopes[h] * rel * dilation, NEG_BIG) for h in range(n_heads)]
        cases.append(np.stack([np.concatenate(per_head[2 * p:2 * p + 2], axis=0) for p in range(n_heads // 2)]))
    return jnp.asarray(np.stack(cases), dtype=F32)


@functools.lru_cache(maxsize=None)
def _head_expand(n_heads):
    e = np.zeros((LANES, n_heads * HEAD_DIM))
    for h in range(n_heads):
        e[h, h * HEAD_DIM:(h + 1) * HEAD_DIM] = 1.0
    return jnp.asarray(e, dtype=F32).astype(BF16)


def _ada_kernel(c_ref, w_ref, b_ref, o_ref):
    o_ref[...] = _dot3(c_ref[...], w_ref[...]) + b_ref[...]


def _ada(c, w_ada, b_ada):
    depth, d, n3 = w_ada.shape
    bsz = c.shape[0]
    tn = d
    return pl.pallas_call(
        _ada_kernel,
        grid=(depth, n3 // tn),
        in_specs=[
            pl.BlockSpec((bsz, d), lambda l, j: (0, 0)),
            pl.BlockSpec((None, d, tn), lambda l, j: (l, 0, j)),
            pl.BlockSpec((None, 1, tn), lambda l, j: (l, 0, j)),
        ],
        out_specs=pl.BlockSpec((None, bsz, tn), lambda l, j: (l, 0, j)),
        out_shape=jax.ShapeDtypeStruct((depth, bsz, n3), F32),
        name="ada_mod",
    )(c, w_ada, b_ada.reshape(depth, 1, n3))


def _inproj_kernel(x_ref, mod_ref, w_ref, b_ref, o_ref, h_ref):
    @pl.when(pl.program_id(2) == 0)
    def _():
        hn = _layer_norm(x_ref[...])
        shift, scale = mod_ref[0:1, :], mod_ref[1:2, :]
        h_ref[...] = (hn * (1.0 + scale) + shift).astype(BF16)

    o_ref[...] = (_mm(h_ref[...], w_ref[...]) + b_ref[...]).astype(o_ref.dtype)


def _inproj(x, mod, w, b):
    bsz, seq, d = x.shape
    n = w.shape[1]
    tm = ROW_TILE
    return pl.pallas_call(
        _inproj_kernel,
        grid=(bsz, seq // tm, n // COL_TILE),
        in_specs=[
            pl.BlockSpec((None, tm, d), lambda bi, i, j: (bi, i, 0)),
            pl.BlockSpec((None, 3, d), lambda bi, i, j: (bi, 0, 0)),
            pl.BlockSpec((d, COL_TILE), lambda bi, i, j: (0, j)),
            pl.BlockSpec((1, COL_TILE), lambda bi, i, j: (0, j)),
        ],
        out_specs=pl.BlockSpec((None, tm, COL_TILE), lambda bi, i, j: (bi, i, j)),
        out_shape=jax.ShapeDtypeStruct((bsz, seq, n), BF16),
        scratch_shapes=[pltpu.VMEM((tm, d), BF16)],
        compiler_params=pltpu.CompilerParams(dimension_semantics=("parallel", "parallel", "arbitrary")),
        name="inproj",
    )(x, mod, w, b)


def _attn_kernel(q_ref, k_ref, v_ref, bias_ref, o_ref, lse_ref, *, length, radius, n_pairs):
    tq, tk = 2 * radius, 4 * radius
    rows = q_ref.shape[0]
    base = pl.program_id(2) * rows
    lane = lax.broadcasted_iota(jnp.int32, (tq, LANES), 1)
    first_head = lane < HEAD_DIM

    def block(jb, carry):
        q0l = pl.multiple_of(jb * tq, tq)
        q0 = base + q0l
        ks = pl.multiple_of(jnp.clip(q0 - radius, 0, length - tk), radius)
        case = jnp.where(q0 == 0, 0, jnp.where(q0 == length - tq, 2, 1))
        lse_tile = jnp.zeros((tq, LANES), F32)
        for p in range(n_pairs):
            cols = slice(p * LANES, (p + 1) * LANES)
            q2 = q_ref[pl.ds(q0l, tq), cols] * (HEAD_DIM ** -0.5)
            k2 = k_ref[pl.ds(ks, tk), cols]
            v2 = v_ref[pl.ds(ks, tk), cols]
            zero = jnp.zeros_like(q2)
            qs = jnp.concatenate([jnp.where(first_head, q2, zero), jnp.where(first_head, zero, q2)], axis=0)
            s = lax.dot_general(qs, k2, (((1,), (1,)), ((), ())), preferred_element_type=F32)
            s = s + bias_ref[case, p]
            m = jnp.max(s, axis=1, keepdims=True)
            e = jnp.exp(s - m)
            l = jnp.sum(e, axis=1, keepdims=True)
            eb = e.astype(BF16)
            oa = _mm(eb[:tq], v2) / l[:tq]
            ob = _mm(eb[tq:], v2) / l[tq:]
            o_ref[pl.ds(q0l, tq), cols] = jnp.where(first_head, oa, ob).astype(o_ref.dtype)
            lse = m + jnp.log(l)
            lse_tile = jnp.where(lane == 2 * p, lse[:tq], lse_tile)
            lse_tile = jnp.where(lane == 2 * p + 1, lse[tq:], lse_tile)
        lse_ref[pl.ds(q0l, tq), :] = lse_tile
        return carry

    lax.fori_loop(0, rows // tq, block, 0)


def _attention(qkv, col_block0, dilation, radius, n_heads):
    bsz, d, length, _ = qkv.shape
    da = n_heads * HEAD_DIM
    rows = min(ROW_TILE, length)
    bias = _attn_bias(n_heads, dilation, radius)
    kern = functools.partial(_attn_kernel, length=length, radius=radius, n_pairs=n_heads // 2)
    return pl.pallas_call(
        kern,
        grid=(bsz, d, length // rows),
        in_specs=[
            pl.BlockSpec((None, None, rows, da), lambda bi, r, i: (bi, r, i, col_block0)),
            pl.BlockSpec((None, None, length, da), lambda bi, r, i: (bi, r, 0, col_block0 + 1)),
            pl.BlockSpec((None, None, length, da), lambda bi, r, i: (bi, r, 0, col_block0 + 2)),
            pl.BlockSpec(bias.shape, lambda bi, r, i: (0, 0, 0, 0)),
        ],
        out_specs=[
            pl.BlockSpec((None, None, rows, da), lambda bi, r, i: (bi, r, i, 0)),
            pl.BlockSpec((None, None, rows, LANES), lambda bi, r, i: (bi, r, i, 0)),
        ],
        out_shape=[
            jax.ShapeDtypeStruct((bsz, d, length, da), BF16),
            jax.ShapeDtypeStruct((bsz, d, length, LANES), F32),
        ],
        name=f"attn_d{dilation}",
    )(qkv, qkv, qkv, bias)


def _to_residue_major(a, dilation):
    bsz, seq, n = a.shape
    return a.reshape(bsz, seq // dilation, dilation, n).transpose(0, 2, 1, 3)


def _to_natural(a):
    bsz, d, length, n = a.shape
    return a.transpose(0, 2, 1, 3).reshape(bsz, d * length, n)


def _filter_kernel(pos_ref, w1_ref, b1_ref, w2_ref, b2_ref, w3_ref, b3_ref, w4_ref, fr_ref, dec_ref, o_ref):
    fr = fr_ref[...]
    h = jnp.sin(fr * (_dot3(pos_ref[...], w1_ref[...]) + b1_ref[...]))
    h = jnp.sin(fr * (_dot3(h, w2_ref[...]) + b2_ref[...]))
    h = jnp.sin(fr * (_dot3(h, w3_ref[...]) + b3_ref[...]))
    f = _dot3(h, w4_ref[...])
    dec = dec_ref[...]
    c = dec.shape[1]
    row = pl.program_id(0) * pos_ref.shape[0] + lax.broadcasted_iota(jnp.int32, (pos_ref.shape[0], 1), 0)
    for o in range(HYENA_ORDER):
        fwd = f[:, (2 * o) * c:(2 * o + 1) * c] * dec
        bwd = jnp.where(row == 0, 0.0, f[:, (2 * o + 1) * c:(2 * o + 2) * c] * dec)
        o_ref[:, (2 * o) * c:(2 * o + 1) * c] = fwd + bwd
        o_ref[:, (2 * o + 1) * c:(2 * o + 2) * c] = bwd - fwd


def _filters(seq, c, w1, b1, w2, b2, w3, b3, w4, freq):
    pos, decay = _filter_consts(seq, c)
    tm = 512
    w1p = jnp.pad(w1, ((0, PAD_EMB - FILTER_EMB), (0, 0)))
    full = lambda a: pl.BlockSpec(a.shape, lambda i: (0,) * a.ndim)
    args = (w1p, b1.reshape(1, -1), w2, b2.reshape(1, -1), w3, b3.reshape(1, -1), w4, freq.reshape(1, -1))
    return pl.pallas_call(
        _filter_kernel,
        grid=(seq // tm,),
        in_specs=[pl.BlockSpec((tm, PAD_EMB), lambda i: (i, 0))] + [full(a) for a in args]
        + [pl.BlockSpec((tm, c), lambda i: (i, 0))],
        out_specs=pl.BlockSpec((tm, 2 * HYENA_ORDER * c), lambda i: (i, 0)),
        out_shape=jax.ShapeDtypeStruct((seq, 2 * HYENA_ORDER * c), F32),
        name="hyena_filters",
    )(pos, *args, decay)


def _store_spectrum(y_ref, l, y):
    half, c = y.shape[0] // 2, y.shape[1]
    y_ref[l, :, 0:c] = y[:half].astype(y_ref.dtype)
    y_ref[l, :, c:2 * c] = y[half:].astype(y_ref.dtype)


def _load_spectrum(blk):
    c = blk.shape[1] // 2
    return jnp.concatenate([blk[:, :c], blk[:, c:]], axis=0)


def _fstage1_kernel(f_ref, f1r_ref, y_ref):
    for l in range(f_ref.shape[0]):
        _store_spectrum(y_ref, l, _mm(f1r_ref[...], f_ref[l].astype(BF16)))


def _fstage2_kernel(e_ref, d_ref, g_ref, h_ref):
    for l in range(e_ref.shape[0]):
        h_ref[l, 0:NB, :] = _mm(g_ref[l, 0:NB, :], _load_spectrum(e_ref[l])).astype(h_ref.dtype)
        h_ref[l, NB:2 * NB, :] = (-_mm(g_ref[l, NB:2 * NB, :], _load_spectrum(d_ref[l]))).astype(h_ref.dtype)


def _filter_spectra(filt, seq, c):
    _, f1r, _, g, _ = _dft_consts(seq)
    na_full = 2 * seq // NB
    na_half = na_full // 2
    n_cb = 2 * HYENA_ORDER
    fv = filt.reshape(na_half, NB, n_cb * c).transpose(1, 0, 2)
    yf = pl.pallas_call(
        _fstage1_kernel,
        grid=(n_cb, NB // NB_STEP),
        in_specs=[pl.BlockSpec((NB_STEP, na_half, c), lambda cb, s: (s, 0, cb)),
                  pl.BlockSpec(f1r.shape, lambda cb, s: (0, 0))],
        out_specs=pl.BlockSpec((None, NB_STEP, na_full, 2 * c), lambda cb, s: (cb, s, 0, 0)),
        out_shape=jax.ShapeDtypeStruct((n_cb, NB, na_full, 2 * c), BF16),
        name="hyena_filter_stage1",
    )(fv, f1r)
    yf = yf.transpose(0, 2, 1, 3)
    return pl.pallas_call(
        _fstage2_kernel,
        grid=(HYENA_ORDER, na_full // KA_STEP),
        in_specs=[pl.BlockSpec((None, KA_STEP, NB, 2 * c), lambda o, s: (2 * o, s, 0, 0)),
                  pl.BlockSpec((None, KA_STEP, NB, 2 * c), lambda o, s: (2 * o + 1, s, 0, 0)),
                  pl.BlockSpec((KA_STEP, 2 * NB, 2 * NB), lambda o, s: (s, 0, 0))],
        out_specs=pl.BlockSpec((None, KA_STEP, 2 * NB, c), lambda o, s: (o, s, 0, 0)),
        out_shape=jax.ShapeDtypeStruct((HYENA_ORDER, na_full, 2 * NB, c), BF16),
        name="hyena_filter_stage2",
    )(yf, yf, g)


def _hyena_schedule(seq):
    nt, nk = seq // HY_ROWS, (2 * seq // NB) // HY_KA
    s = {"load": 0, "fwd0": nt, "spec0": nt + 1, "inv0": nt + 1 + nk, "mid": nt + 2 + nk}
    s.update(fwd1=s["mid"] + nt, spec1=s["mid"] + nt + 1, inv1=s["mid"] + nt + 1 + nk, last=s["mid"] + nt + 2 + nk)
    s.update(total=s["last"] + nt, nt=nt, nk=nk)
    return s


def _hyena_kernel(p_ref, prev_ref, next_ref, gh_ref, cw_ref, cb_ref, fb_ref, hf_ref, g_ref, gi_ref, f1_ref, f4_ref,
                  o_ref, v_buf, y_buf, s_buf, *, seq):
    sch = _hyena_schedule(seq)
    t = pl.program_id(2)
    n_slab = HY_LANES // LANES
    na_half = seq // NB
    na_full, pair_rows = 2 * na_half, 2 * na_half
    groups = HY_ROWS // NB
    lanes = lambda s: slice(s * LANES, (s + 1) * LANES)
    row = lax.broadcasted_iota(jnp.int32, (HY_ROWS, 1), 0)

    def short_conv(tile, b):
        a = p_ref[b].astype(F32)
        top = jnp.where(tile > 0, prev_ref[b, SUBLANES - 1:SUBLANES, :].astype(F32), 0.0)
        bot = jnp.where(tile < sch["nt"] - 1, next_ref[b, 0:1, :].astype(F32), 0.0)
        up = jnp.where(row == 0, top, pltpu.roll(a, 1, 0))
        dn = jnp.where(row == HY_ROWS - 1, bot, pltpu.roll(a, HY_ROWS - 1, 0))
        return cw_ref[0:1, :] * up + cw_ref[1:2, :] * a + cw_ref[2:3, :] * dn + cb_ref[...]

    def time_rows(b, tile, j):
        return pl.ds(pl.multiple_of((b * na_half + tile * groups + j) * PITCH_T, SUBLANES), NB)

    def forward_stage1():
        def body(nb, carry):
            rhs = jnp.concatenate([v_buf[s, pl.ds(nb, pair_rows, stride=PITCH_T), :] for s in range(n_slab)], axis=1)
            y = _mm(f1_ref[...], rhs.astype(BF16))
            for ri in range(2):
                for s in range(n_slab):
                    s_buf[s, pl.ds(ri * NB + nb, na_full, stride=PITCH_F), :] = y[ri * na_full:(ri + 1) * na_full, lanes(s)]
            return carry
        lax.fori_loop(0, NB, body, 0)

    def spectral(step):
        def body(l, carry):
            rows = pl.ds(pl.multiple_of((step * HY_KA + l) * PITCH_F, SUBLANES), 2 * NB)
            rhs = jnp.concatenate([s_buf[s, rows, :] for s in range(n_slab)], axis=1)
            x = _mm(g_ref[l], rhs.astype(BF16))
            xr, xi = x[:NB], x[NB:]
            hr, hi = hf_ref[l, 0:NB, :].astype(F32), hf_ref[l, NB:2 * NB, :].astype(F32)
            prod = jnp.concatenate([xr * hr - xi * hi, xr * hi + xi * hr], axis=0)
            tt = _mm(gi_ref[l], prod.astype(BF16))
            for s in range(n_slab):
                s_buf[s, rows, :] = tt[:, lanes(s)]
            return carry
        lax.fori_loop(0, HY_KA, body, 0)

    def inverse_stage():
        def body(nb, carry):
            parts = [jnp.concatenate([s_buf[s, pl.ds(ri * NB + nb, na_full, stride=PITCH_F), :] for s in range(n_slab)],
                                     axis=1) for ri in range(2)]
            y = _mm(f4_ref[...], jnp.concatenate(parts, axis=0).astype(BF16))
            for s in range(n_slab):
                y_buf[s, pl.ds(nb, pair_rows, stride=PITCH_T), :] = y[:, lanes(s)]
            return carry
        lax.fori_loop(0, NB, body, 0)

    @pl.when(t < sch["fwd0"])
    def _():
        for b in range(2):
            v = short_conv(t, b)
            for j in range(groups):
                for s in range(n_slab):
                    v_buf[s, time_rows(b, t, j), :] = v[j * NB:(j + 1) * NB, lanes(s)]

    @pl.when((t == sch["fwd0"]) | (t == sch["fwd1"]))
    def _():
        forward_stage1()

    @pl.when((t >= sch["spec0"]) & (t < sch["inv0"]))
    def _():
        spectral(t - sch["spec0"])

    @pl.when((t >= sch["spec1"]) & (t < sch["inv1"]))
    def _():
        spectral(t - sch["spec1"])

    @pl.when((t == sch["inv0"]) | (t == sch["inv1"]))
    def _():
        inverse_stage()

    @pl.when((t >= sch["mid"]) & (t < sch["fwd1"]))
    def _():
        tile = t - sch["mid"]
        for b in range(2):
            x1 = short_conv(tile, b)
            for j in range(groups):
                for s in range(n_slab):
                    rows = time_rows(b, tile, j)
                    z = x1[j * NB:(j + 1) * NB, lanes(s)] * (y_buf[s, rows, :] + fb_ref[:, lanes(s)] * v_buf[s, rows, :])
                    v_buf[s, rows, :] = z

    @pl.when(t >= sch["last"])
    def _():
        tile = t - sch["last"]
        for b in range(2):
            x2 = short_conv(tile, b) * _silu(gh_ref[b].astype(F32))
            for j in range(groups):
                for s in range(n_slab):
                    rows = time_rows(b, tile, j)
                    z = x2[j * NB:(j + 1) * NB, lanes(s)] * (y_buf[s, rows, :] + fb_ref[:, lanes(s)] * v_buf[s, rows, :])
                    o_ref[b, j * NB:(j + 1) * NB, lanes(s)] = z.astype(o_ref.dtype)


def _hyena(p, conv_w, conv_b, filt_bias, hf, c):
    bsz, seq, ncols = p.shape
    f1, _, f4, g, gi = _dft_consts(seq)
    sch = _hyena_schedule(seq)
    nt, nk = sch["nt"], sch["nk"]
    n_pairs, na_full = bsz // 2, 2 * seq // NB
    groups_per_c = c // HY_LANES
    halo_per_tile = HY_ROWS // SUBLANES
    pv = p.reshape(n_pairs, 2, seq, ncols)

    def tile_of(t):
        return jnp.where(t < sch["mid"], jnp.minimum(t, nt - 1),
                         jnp.where(t < sch["last"], jnp.clip(t - sch["mid"], 0, nt - 1), jnp.clip(t - sch["last"], 0, nt - 1)))

    def col_of(t, lg):
        return jnp.where(t < sch["mid"], 0, jnp.where(t < sch["last"], 1, 2)) * groups_per_c + lg

    def freq_of(t):
        return jnp.where(t < sch["fwd1"], jnp.clip(t - sch["spec0"], 0, nk - 1), jnp.clip(t - sch["spec1"], 0, nk - 1))

    order_of = lambda t: jnp.where(t < sch["fwd1"], 0, 1)
    last_tile = lambda t: jnp.clip(t - sch["last"], 0, nt - 1)
    const = lambda a: pl.BlockSpec(a.shape, lambda bp, lg, t: (0,) * a.ndim)
    width = 3 * c
    out = pl.pallas_call(
        functools.partial(_hyena_kernel, seq=seq),
        grid=(n_pairs, groups_per_c, sch["total"]),
        in_specs=[
            pl.BlockSpec((None, 2, HY_ROWS, HY_LANES), lambda bp, lg, t: (bp, 0, tile_of(t), col_of(t, lg))),
            pl.BlockSpec((None, 2, SUBLANES, HY_LANES),
                         lambda bp, lg, t: (bp, 0, jnp.maximum(tile_of(t) * halo_per_tile - 1, 0), col_of(t, lg))),
            pl.BlockSpec((None, 2, SUBLANES, HY_LANES),
                         lambda bp, lg, t: (bp, 0, jnp.minimum((tile_of(t) + 1) * halo_per_tile, seq // SUBLANES - 1),
                                            col_of(t, lg))),
            pl.BlockSpec((None, 2, HY_ROWS, HY_LANES), lambda bp, lg, t: (bp, 0, last_tile(t), 3 * groups_per_c + lg)),
            pl.BlockSpec((3, HY_LANES), lambda bp, lg, t: (0, col_of(t, lg))),
            pl.BlockSpec((1, HY_LANES), lambda bp, lg, t: (0, col_of(t, lg))),
            pl.BlockSpec((None, 1, HY_LANES), lambda bp, lg, t: (order_of(t), 0, lg)),
            pl.BlockSpec((None, HY_KA, 2 * NB, HY_LANES), lambda bp, lg, t: (order_of(t), freq_of(t), 0, lg)),
            pl.BlockSpec((HY_KA, 2 * NB, 2 * NB), lambda bp, lg, t: (freq_of(t), 0, 0)),
            pl.BlockSpec((HY_KA, 2 * NB, 2 * NB), lambda bp, lg, t: (freq_of(t), 0, 0)),
            const(f1), const(f4),
        ],
        out_specs=pl.BlockSpec((None, 2, HY_ROWS, HY_LANES), lambda bp, lg, t: (bp, 0, last_tile(t), lg)),
        out_shape=jax.ShapeDtypeStruct((n_pairs, 2, seq, c), BF16),
        scratch_shapes=[
            pltpu.VMEM((HY_LANES // LANES, na_full * PITCH_T, LANES), F32),
            pltpu.VMEM((HY_LANES // LANES, na_full * PITCH_T, LANES), F32),
            pltpu.VMEM((HY_LANES // LANES, na_full * PITCH_F, LANES), F32),
        ],
        compiler_params=pltpu.CompilerParams(dimension_semantics=("parallel", "parallel", "arbitrary")),
        name="hyena_long_conv",
    )(pv, pv, pv, pv, conv_w, conv_b.reshape(1, width), filt_bias.reshape(HYENA_ORDER, 1, c), hf, g, gi, f1, f4)
    return out.reshape(bsz, seq, c)


def _out_kernel(o0_ref, o1_ref, o2_ref, l0_ref, l1_ref, l2_ref, ga_ref, yh_ref, ra_ref, rh_ref, x_ref, mod_ref,
                e_ref, wpa_ref, wph_ref, wo_ref, bo_ref, lg_ref, lb_ref, out_ref, *, alpha):
    lses = [l0_ref[...], l1_ref[...], l2_ref[...]]
    m = jnp.maximum(jnp.maximum(lses[0], lses[1]), lses[2])
    es = [jnp.exp(l - m) for l in lses]
    den = es[0] + es[1] + es[2]
    o = None
    for e, o_ref in zip(es, (o0_ref, o1_ref, o2_ref)):
        hi, lo = _split(e / den)
        w = _mm(hi, e_ref[...]) + _mm(lo, e_ref[...])
        term = w * o_ref[...].astype(F32)
        o = term if o is None else o + term
    ya = (o * _silu(ga_ref[...].astype(F32))).astype(BF16)
    merged = (jax.nn.sigmoid(ra_ref[...].astype(F32)) * _mm(ya, wpa_ref[...])
              + jax.nn.sigmoid(rh_ref[...].astype(F32)) * _mm(yh_ref[...], wph_ref[...]))
    out = _mm(merged.astype(BF16), wo_ref[...]) + bo_ref[...]
    res = alpha * x_ref[...] + mod_ref[2:3, :] * out
    out_ref[...] = _layer_norm(res) * lg_ref[...] + lb_ref[...]


def _out_proj(os_, lses, p, cols, yh, x, mod, wpa, wph, wo, bo, lg, lb, alpha):
    bsz, seq, d = x.shape
    da = wpa.shape[0]
    tm = OUT_TILE
    e = _head_expand(da // HEAD_DIM)
    rows = lambda width, cb=0: pl.BlockSpec((None, tm, width), lambda bi, i: (bi, i, cb))
    const = lambda a: pl.BlockSpec(a.shape, lambda bi, i: (0,) * a.ndim)
    ga_col, ra_col, rh_col = cols
    return pl.pallas_call(
        functools.partial(_out_kernel, alpha=alpha),
        grid=(bsz, seq // tm),
        in_specs=[rows(da)] * 3 + [rows(LANES)] * 3
        + [rows(da, ga_col // da), rows(da), rows(d, ra_col // d), rows(d, rh_col // d), rows(d),
           pl.BlockSpec((None, 3, d), lambda bi, i: (bi, 0, 0)),
           const(e), const(wpa), const(wph), const(wo)]
        + [pl.BlockSpec((1, d), lambda bi, i: (0, 0))] * 3,
        out_specs=rows(d),
        out_shape=jax.ShapeDtypeStruct((bsz, seq, d), F32),
        name="merge_out_proj",
    )(*os_, *lses, p, yh, p, p, x, mod, e, wpa, wph, wo, bo.reshape(1, d), lg.reshape(1, d), lb.reshape(1, d))


def _layer(x, mod, w_in, b_in, conv_w, conv_b, fw1, fb1, fw2, fb2, fw3, fb3, fw4, ffreq, fbias,
           w_proj_attn, w_proj_hyena, w_out, b_out, ln_g, ln_b, alpha):
    bsz, seq, d = x.shape
    da = w_proj_attn.shape[0]
    c = w_proj_hyena.shape[0]
    n_heads = da // HEAD_DIM

    def qkv_cols(a, g):
        return [a[..., (i * N_GROUPS + g) * da:(i * N_GROUPS + g + 1) * da] for i in range(3)]

    i1 = 3 * N_GROUPS * da
    i2 = i1 + da
    i4 = i2 + 4 * c

    def regroup(a):
        parts = [a[..., i2:i4]] + qkv_cols(a, 0) + [a[..., i1:i2], a[..., i4:]]
        for g in range(1, N_GROUPS):
            parts += qkv_cols(a, g)
        return jnp.concatenate(parts, axis=-1)

    col_qkv0 = 4 * c
    col_ga = col_qkv0 + 3 * da
    col_ra = col_ga + da
    col_rh = col_ra + d
    col_qkv = [col_qkv0] + [col_rh + d + (g - 1) * 3 * da for g in range(1, N_GROUPS)]

    p = _inproj(x, mod, regroup(w_in).astype(BF16), regroup(b_in).reshape(1, -1))

    outs, lses = [], []
    for g, (window, dilation) in enumerate(DILATED_GROUPS):
        radius = window // (2 * dilation)
        if dilation == 1:
            o_g, lse_g = _attention(p.reshape(bsz, 1, seq, p.shape[-1]), col_qkv[g] // da, 1, radius, n_heads)
        else:
            qkv = _to_residue_major(p[..., col_qkv[g]:col_qkv[g] + 3 * da], dilation)
            o_g, lse_g = _attention(qkv, 0, dilation, radius, n_heads)
        outs.append(_to_natural(o_g))
        lses.append(_to_natural(lse_g))

    filt = _filters(seq, c, fw1, fb1, fw2, fb2, fw3, fb3, fw4, ffreq)
    hf = _filter_spectra(filt, seq, c)
    yh = _hyena(p, conv_w, conv_b, fbias, hf, c)

    return _out_proj(outs, lses, p, (col_ga, col_ra, col_rh), yh, x, mod,
                     w_proj_attn.astype(BF16), w_proj_hyena.astype(BF16), w_out.astype(BF16),
                     b_out, ln_g, ln_b, alpha)


def kernel(x, c, w_ada, b_ada, w_in, b_in, conv_w, conv_b, filt_w1, filt_b1, filt_w2, filt_b2, filt_w3, filt_b3,
           filt_w4, filt_freq, filt_bias, w_proj_attn, w_proj_hyena, w_out, b_out, ln_g, ln_b):
    depth = w_ada.shape[0]
    alpha = (2 * depth) ** 0.25
    d = x.shape[-1]
    mods = _ada(c, w_ada, b_ada).reshape(depth, c.shape[0], 3, d)
    for l in range(depth):
        x = _layer(x, mods[l], w_in[l], b_in[l], conv_w[l], conv_b[l], filt_w1[l], filt_b1[l], filt_w2[l],
                   filt_b2[l], filt_w3[l], filt_b3[l], filt_w4[l], filt_freq[l], filt_bias[l],
                   w_proj_attn[l], w_proj_hyena[l], w_out[l], b_out[l], ln_g[l], ln_b[l], alpha)
    return x
```

```python
import functools
import math

import numpy as np
import jax
import jax.numpy as jnp
from jax import lax
from jax.experimental import pallas as pl
from jax.experimental.pallas import tpu as pltpu

F32 = jnp.float32
BF16 = jnp.bfloat16

HEAD_DIM = 64
DILATED_GROUPS = ((128, 1), (512, 4), (2048, 16))
N_GROUPS = len(DILATED_GROUPS)
HYENA_ORDER = 2
FILTER_EMB = 33
FILTER_WIDTH = 64
DECAY_TARGET = 1e-2
FAST_DECAY_PCT = 0.3
SLOW_DECAY_PCT = 1.5
LN_EPS = 1e-5

LANES = 128
NB = 64
PAD_EMB = 128
NEG_BIG = -1e30

ROW_TILE = 1024
COL_TILE = 1536
OUT_TILE = 512
ATTN_UNROLL = 4
NB_STEP = 8
KA_STEP = 8
HY_LANES = 256
HY_ROWS = 1024
HY_KA = 16
HY_UNROLL = 8
SUBLANES = 8
PITCH_T = NB + SUBLANES
PITCH_F = 2 * NB + SUBLANES


def _mm(a, b):
    return jnp.dot(a, b, preferred_element_type=F32)


def _split(a):
    hi = a.astype(BF16)
    lo = (a - hi.astype(F32)).astype(BF16)
    return hi, lo


def _dot3(a, b):
    ah, al = _split(a)
    bh, bl = _split(b)
    return _mm(ah, bh) + _mm(ah, bl) + _mm(al, bh)


def _layer_norm(x):
    mu = jnp.mean(x, axis=-1, keepdims=True)
    xc = x - mu
    var = jnp.mean(xc * xc, axis=-1, keepdims=True)
    return xc * lax.rsqrt(var + LN_EPS)


def _silu(a):
    return a * jax.nn.sigmoid(a)


@functools.lru_cache(maxsize=None)
def _dft_consts(seq):
    n = 2 * seq
    na_full = n // NB
    na_half = na_full // 2
    ka = np.arange(na_full)[:, None]
    na = np.arange(na_half)[None, :]
    ang = 2.0 * np.pi * ((ka * na) % na_full) / na_full
    c, s = np.cos(ang), np.sin(ang)
    f1 = np.block([[c, s], [-s, c]])
    f1r = np.concatenate([c, -s], axis=0)
    f4 = np.block([[c.T, -s.T], [s.T, c.T]]) / n
    kav = np.arange(na_full)[:, None, None]
    kb = np.arange(NB)[None, :, None]
    nb = np.arange(NB)[None, None, :]
    th = 2.0 * np.pi * (((nb * kav) % n) / n + ((nb * kb) % NB) / NB)
    cg, sg = np.cos(th), np.sin(th)
    g = np.concatenate([np.concatenate([cg, sg], axis=2), np.concatenate([-sg, cg], axis=2)], axis=1)
    cgt, sgt = cg.transpose(0, 2, 1), sg.transpose(0, 2, 1)
    gi = np.concatenate([np.concatenate([cgt, -sgt], axis=2), np.concatenate([sgt, cgt], axis=2)], axis=1)
    as_bf16 = lambda a: jnp.asarray(a, dtype=F32).astype(BF16)
    return as_bf16(f1), as_bf16(f1r), as_bf16(f4), as_bf16(g), as_bf16(gi)


@functools.lru_cache(maxsize=None)
def _filter_consts(seq, channels):
    t = np.linspace(0.0, 1.0, seq)[:, None]
    bands = (FILTER_EMB - 1) // 2
    w = 2.0 * np.pi * np.arange(seq)[:, None] / seq
    f = np.linspace(1e-4, bands - 1, bands)[None, :]
    pos = np.concatenate([t, np.cos(f * w), -np.sin(f * w)], axis=-1)
    pos = np.pad(pos, ((0, 0), (0, PAD_EMB - FILTER_EMB)))
    deltas = np.linspace(math.log(DECAY_TARGET) / SLOW_DECAY_PCT, math.log(DECAY_TARGET) / FAST_DECAY_PCT, channels)
    decay = np.exp(-t * np.abs(deltas)[None, :])
    return jnp.asarray(pos, dtype=F32), jnp.asarray(decay, dtype=F32)


@functools.lru_cache(maxsize=None)
def _attn_bias(n_heads, dilation, radius):
    tq, tk = 2 * radius, 4 * radius
    slopes = 2.0 ** (-8.0 * (np.arange(n_heads) + 1) / n_heads)
    i = np.arange(tq)[:, None]
    j = np.arange(tk)[None, :]
    cases = []
    for off in (0, radius, 2 * radius):
        rel = np.abs(j - off - i)
        per_head = [np.where(rel <= radius, -slopes[h] * rel * dilation, NEG_BIG) for h in range(n_heads)]
        cases.append(np.stack([np.concatenate(per_head[2 * p:2 * p + 2], axis=0) for p in range(n_heads // 2)]))
    return jnp.asarray(np.stack(cases), dtype=F32)


def _ada_kernel(c_ref, w_ref, b_ref, o_ref):
    o_ref[...] = _dot3(c_ref[...], w_ref[...]) + b_ref[...]


def _ada(c, w_ada, b_ada):
    depth, d, n3 = w_ada.shape
    bsz = c.shape[0]
    tn = d
    return pl.pallas_call(
        _ada_kernel,
        grid=(depth, n3 // tn),
        in_specs=[
            pl.BlockSpec((bsz, d), lambda l, j: (0, 0)),
            pl.BlockSpec((None, d, tn), lambda l, j: (l, 0, j)),
            pl.BlockSpec((None, 1, tn), lambda l, j: (l, 0, j)),
        ],
        out_specs=pl.BlockSpec((None, bsz, tn), lambda l, j: (l, 0, j)),
        out_shape=jax.ShapeDtypeStruct((depth, bsz, n3), F32),
        name="ada_mod",
    )(c, w_ada, b_ada.reshape(depth, 1, n3))


def _inproj_kernel(x_ref, mod_ref, w_ref, b_ref, o_ref, h_ref):
    @pl.when(pl.program_id(2) == 0)
    def _():
        hn = _layer_norm(x_ref[...])
        shift, scale = mod_ref[0:1, :], mod_ref[1:2, :]
        h_ref[...] = (hn * (1.0 + scale) + shift).astype(BF16)

    o_ref[...] = (_mm(h_ref[...], w_ref[...]) + b_ref[...]).astype(o_ref.dtype)


def _inproj(x, mod, w, b):
    bsz, seq, d = x.shape
    n = w.shape[1]
    tm = ROW_TILE
    return pl.pallas_call(
        _inproj_kernel,
        grid=(bsz, seq // tm, n // COL_TILE),
        in_specs=[
            pl.BlockSpec((None, tm, d), lambda bi, i, j: (bi, i, 0)),
            pl.BlockSpec((None, 3, d), lambda bi, i, j: (bi, 0, 0)),
            pl.BlockSpec((d, COL_TILE), lambda bi, i, j: (0, j)),
            pl.BlockSpec((1, COL_TILE), lambda bi, i, j: (0, j)),
        ],
        out_specs=pl.BlockSpec((None, tm, COL_TILE), lambda bi, i, j: (bi, i, j)),
        out_shape=jax.ShapeDtypeStruct((bsz, seq, n), BF16),
        scratch_shapes=[pltpu.VMEM((tm, d), BF16)],
        compiler_params=pltpu.CompilerParams(dimension_semantics=("parallel", "parallel", "arbitrary")),
        name="inproj",
    )(x, mod, w, b)


def _attn_kernel(q_ref, k_ref, v_ref, bias_ref, o_ref, lse_ref, *scratch, dilation, seq, radius):
    tq, tk = 2 * radius, 4 * radius
    d = dilation
    length = seq // d
    nblk = length // tq
    first_head = lax.broadcasted_iota(jnp.int32, (tq, LANES), 1) < HEAD_DIM
    first_head_k = lax.broadcasted_iota(jnp.int32, (tk, LANES), 1) < HEAD_DIM
    if d > 1:
        qf, kf, vf, of, lf = scratch
        qf[...] = q_ref[...].astype(F32)
        kf[...] = k_ref[...].astype(F32)
        vf[...] = v_ref[...].astype(F32)

    def rows(r, start, n):
        if d == 1:
            return pl.ds(pl.multiple_of(start, radius), n)
        return pl.ds(r + d * start, n, stride=d)

    def block(i, carry):
        r, jb = lax.shift_right_logical(i, nblk.bit_length() - 1), i & (nblk - 1)
        q0 = jb * tq
        ks = jnp.clip(q0 - radius, 0, length - tk)
        case = jnp.where(jb == 0, 0, jnp.where(jb == nblk - 1, 2, 1))
        if d == 1:
            q2, k2, v2 = q_ref[rows(r, q0, tq), :], k_ref[rows(r, ks, tk), :], v_ref[rows(r, ks, tk), :]
        else:
            q2, k2, v2 = (ref[rows(r, st, n), :].astype(BF16) for ref, st, n in ((qf, q0, tq), (kf, ks, tk), (vf, ks, tk)))
        q2 = q2 * (HEAD_DIM ** -0.5)
        zero, one = jnp.zeros_like(q2), jnp.ones_like(v2)
        qs = jnp.concatenate([jnp.where(first_head, q2, zero), jnp.where(first_head, zero, q2)], axis=0)
        s = lax.dot_general(qs, k2, (((1,), (1,)), ((), ())), preferred_element_type=F32) + bias_ref[case]
        m = jnp.max(s, axis=1, keepdims=True)
        e = jnp.exp(s - m).astype(BF16)
        oa = _mm(e[:tq], jnp.where(first_head_k, v2, one))
        ob = _mm(e[tq:], jnp.where(first_head_k, one, v2))
        den = jnp.where(first_head, pltpu.roll(oa, HEAD_DIM, 1), pltpu.roll(ob, HEAD_DIM, 1))
        o = jnp.where(first_head, oa, ob) / den
        lse = jnp.where(first_head, m[:tq], m[tq:]) + jnp.log(den)
        if d == 1:
            o_ref[rows(r, q0, tq), :] = o.astype(o_ref.dtype)
            lse_ref[rows(r, q0, tq), :] = lse
        else:
            of[rows(r, q0, tq), :] = o
            lf[rows(r, q0, tq), :] = lse
        return carry

    lax.fori_loop(0, d * nblk, block, 0, unroll=ATTN_UNROLL)
    if d > 1:
        o_ref[...] = of[...].astype(o_ref.dtype)
        lse_ref[...] = lf[...]


def _attention(p, col0, dilation, radius, n_heads):
    bsz, seq, _ = p.shape
    da = n_heads * HEAD_DIM
    n_pairs = da // LANES
    bias = _attn_bias(n_heads, dilation, radius)
    kern = functools.partial(_attn_kernel, dilation=dilation, seq=seq, radius=radius)
    col = lambda k: (lambda bi, pr: (bi, 0, (col0 + k * da) // LANES + pr))
    return pl.pallas_call(
        kern,
        grid=(bsz, n_pairs),
        in_specs=[pl.BlockSpec((None, seq, LANES), col(k)) for k in range(3)]
        + [pl.BlockSpec((3, None) + bias.shape[2:], lambda bi, pr: (0, pr, 0, 0))],
        out_specs=[pl.BlockSpec((None, seq, LANES), lambda bi, pr: (bi, 0, pr))] * 2,
        out_shape=[jax.ShapeDtypeStruct((bsz, seq, da), BF16), jax.ShapeDtypeStruct((bsz, seq, da), F32)],
        scratch_shapes=[pltpu.VMEM((seq, LANES), F32)] * (5 if dilation > 1 else 0),
        name=f"attn_d{dilation}",
    )(p, p, p, bias)


def _filter_kernel(pos_ref, w1_ref, b1_ref, w2_ref, b2_ref, w3_ref, b3_ref, w4_ref, fr_ref, dec_ref, o_ref):
    fr = fr_ref[...]
    h = jnp.sin(fr * (_dot3(pos_ref[...], w1_ref[...]) + b1_ref[...]))
    h = jnp.sin(fr * (_dot3(h, w2_ref[...]) + b2_ref[...]))
    h = jnp.sin(fr * (_dot3(h, w3_ref[...]) + b3_ref[...]))
    f = _dot3(h, w4_ref[...])
    dec = dec_ref[...]
    c = dec.shape[1]
    row = pl.program_id(0) * pos_ref.shape[0] + lax.broadcasted_iota(jnp.int32, (pos_ref.shape[0], 1), 0)
    for o in range(HYENA_ORDER):
        fwd = f[:, (2 * o) * c:(2 * o + 1) * c] * dec
        bwd = jnp.where(row == 0, 0.0, f[:, (2 * o + 1) * c:(2 * o + 2) * c] * dec)
        o_ref[:, (2 * o) * c:(2 * o + 1) * c] = fwd + bwd
        o_ref[:, (2 * o + 1) * c:(2 * o + 2) * c] = bwd - fwd


def _filters(seq, c, w1, b1, w2, b2, w3, b3, w4, freq):
    pos, decay = _filter_consts(seq, c)
    tm = 512
    w1p = jnp.pad(w1, ((0, PAD_EMB - FILTER_EMB), (0, 0)))
    full = lambda a: pl.BlockSpec(a.shape, lambda i: (0,) * a.ndim)
    args = (w1p, b1.reshape(1, -1), w2, b2.reshape(1, -1), w3, b3.reshape(1, -1), w4, freq.reshape(1, -1))
    return pl.pallas_call(
        _filter_kernel,
        grid=(seq // tm,),
        in_specs=[pl.BlockSpec((tm, PAD_EMB), lambda i: (i, 0))] + [full(a) for a in args]
        + [pl.BlockSpec((tm, c), lambda i: (i, 0))],
        out_specs=pl.BlockSpec((tm, 2 * HYENA_ORDER * c), lambda i: (i, 0)),
        out_shape=jax.ShapeDtypeStruct((seq, 2 * HYENA_ORDER * c), F32),
        name="hyena_filters",
    )(pos, *args, decay)


def _store_spectrum(y_ref, l, y):
    half, c = y.shape[0] // 2, y.shape[1]
    y_ref[l, :, 0:c] = y[:half].astype(y_ref.dtype)
    y_ref[l, :, c:2 * c] = y[half:].astype(y_ref.dtype)


def _load_spectrum(blk):
    c = blk.shape[1] // 2
    return jnp.concatenate([blk[:, :c], blk[:, c:]], axis=0)


def _fstage1_kernel(f_ref, f1r_ref, y_ref):
    for l in range(f_ref.shape[0]):
        _store_spectrum(y_ref, l, _mm(f1r_ref[...], f_ref[l].astype(BF16)))


def _fstage2_kernel(e_ref, d_ref, g_ref, h_ref):
    for l in range(e_ref.shape[0]):
        h_ref[l, 0:NB, :] = _mm(g_ref[l, 0:NB, :], _load_spectrum(e_ref[l])).astype(h_ref.dtype)
        h_ref[l, NB:2 * NB, :] = (-_mm(g_ref[l, NB:2 * NB, :], _load_spectrum(d_ref[l]))).astype(h_ref.dtype)


def _filter_spectra(filt, seq, c):
    _, f1r, _, g, _ = _dft_consts(seq)
    na_full = 2 * seq // NB
    na_half = na_full // 2
    n_cb = 2 * HYENA_ORDER
    fv = filt.reshape(na_half, NB, n_cb * c).transpose(1, 0, 2)
    yf = pl.pallas_call(
        _fstage1_kernel,
        grid=(n_cb, NB // NB_STEP),
        in_specs=[pl.BlockSpec((NB_STEP, na_half, c), lambda cb, s: (s, 0, cb)),
                  pl.BlockSpec(f1r.shape, lambda cb, s: (0, 0))],
        out_specs=pl.BlockSpec((None, NB_STEP, na_full, 2 * c), lambda cb, s: (cb, s, 0, 0)),
        out_shape=jax.ShapeDtypeStruct((n_cb, NB, na_full, 2 * c), BF16),
        name="hyena_filter_stage1",
    )(fv, f1r)
    yf = yf.transpose(0, 2, 1, 3)
    return pl.pallas_call(
        _fstage2_kernel,
        grid=(HYENA_ORDER, na_full // KA_STEP),
        in_specs=[pl.BlockSpec((None, KA_STEP, NB, 2 * c), lambda o, s: (2 * o, s, 0, 0)),
                  pl.BlockSpec((None, KA_STEP, NB, 2 * c), lambda o, s: (2 * o + 1, s, 0, 0)),
                  pl.BlockSpec((KA_STEP, 2 * NB, 2 * NB), lambda o, s: (s, 0, 0))],
        out_specs=pl.BlockSpec((None, KA_STEP, 2 * NB, c), lambda o, s: (o, s, 0, 0)),
        out_shape=jax.ShapeDtypeStruct((HYENA_ORDER, na_full, 2 * NB, c), BF16),
        name="hyena_filter_stage2",
    )(yf, yf, g)


def _hyena_schedule(seq):
    nt, nk = seq // HY_ROWS, (2 * seq // NB) // HY_KA
    s = {"load": 0, "fwd0": nt, "spec0": nt + 1, "inv0": nt + 1 + nk, "mid": nt + 2 + nk}
    s.update(fwd1=s["mid"] + nt, spec1=s["mid"] + nt + 1, inv1=s["mid"] + nt + 1 + nk, last=s["mid"] + nt + 2 + nk)
    s.update(total=s["last"] + nt, nt=nt, nk=nk)
    return s


def _hyena_kernel(p_ref, prev_ref, next_ref, gh_ref, cw_ref, cb_ref, fb_ref, hf_ref, g_ref, gi_ref, f1_ref, f4_ref,
                  o_ref, v_buf, y_buf, s_buf, *, seq):
    sch = _hyena_schedule(seq)
    t = pl.program_id(2)
    n_slab = HY_LANES // LANES
    na_half = seq // NB
    na_full, pair_rows = 2 * na_half, 2 * na_half
    groups = HY_ROWS // NB
    lanes = lambda s: slice(s * LANES, (s + 1) * LANES)
    row = lax.broadcasted_iota(jnp.int32, (HY_ROWS, 1), 0)

    def short_conv(tile, b):
        a = p_ref[b].astype(F32)
        top = jnp.where(tile > 0, prev_ref[b, SUBLANES - 1:SUBLANES, :].astype(F32), 0.0)
        bot = jnp.where(tile < sch["nt"] - 1, next_ref[b, 0:1, :].astype(F32), 0.0)
        up = jnp.where(row == 0, top, pltpu.roll(a, 1, 0))
        dn = jnp.where(row == HY_ROWS - 1, bot, pltpu.roll(a, HY_ROWS - 1, 0))
        return cw_ref[0:1, :] * up + cw_ref[1:2, :] * a + cw_ref[2:3, :] * dn + cb_ref[...]

    def time_rows(b, tile, j):
        return pl.ds(pl.multiple_of((b * na_half + tile * groups + j) * PITCH_T, SUBLANES), NB)

    def forward_stage1():
        def body(nb, carry):
            rhs = jnp.concatenate([v_buf[s, pl.ds(nb, pair_rows, stride=PITCH_T), :] for s in range(n_slab)], axis=1)
            y = _mm(f1_ref[...], rhs.astype(BF16))
            for ri in range(2):
                for s in range(n_slab):
                    s_buf[s, pl.ds(ri * NB + nb, na_full, stride=PITCH_F), :] = y[ri * na_full:(ri + 1) * na_full, lanes(s)]
            return carry
        lax.fori_loop(0, NB, body, 0, unroll=HY_UNROLL)

    def spectral(step):
        def body(l, carry):
            rows = pl.ds(pl.multiple_of((step * HY_KA + l) * PITCH_F, SUBLANES), 2 * NB)
            rhs = jnp.concatenate([s_buf[s, rows, :] for s in range(n_slab)], axis=1)
            x = _mm(g_ref[l], rhs.astype(BF16))
            xr, xi = x[:NB], x[NB:]
            hr, hi = hf_ref[l, 0:NB, :].astype(F32), hf_ref[l, NB:2 * NB, :].astype(F32)
            prod = jnp.concatenate([xr * hr - xi * hi, xr * hi + xi * hr], axis=0)
            tt = _mm(gi_ref[l], prod.astype(BF16))
            for s in range(n_slab):
                s_buf[s, rows, :] = tt[:, lanes(s)]
            return carry
        lax.fori_loop(0, HY_KA, body, 0, unroll=HY_UNROLL)

    def inverse_stage():
        def body(nb, carry):
            parts = [jnp.concatenate([s_buf[s, pl.ds(ri * NB + nb, na_full, stride=PITCH_F), :] for s in range(n_slab)],
                                     axis=1) for ri in range(2)]
            y = _mm(f4_ref[...], jnp.concatenate(parts, axis=0).astype(BF16))
            for s in range(n_slab):
                y_buf[s, pl.ds(nb, pair_rows, stride=PITCH_T), :] = y[:, lanes(s)]
            return carry
        lax.fori_loop(0, NB, body, 0, unroll=HY_UNROLL)

    @pl.when(t < sch["fwd0"])
    def _():
        for b in range(2):
            v = short_conv(t, b)
            for j in range(groups):
                for s in range(n_slab):
                    v_buf[s, time_rows(b, t, j), :] = v[j * NB:(j + 1) * NB, lanes(s)]

    @pl.when((t == sch["fwd0"]) | (t == sch["fwd1"]))
    def _():
        forward_stage1()

    @pl.when((t >= sch["spec0"]) & (t < sch["inv0"]))
    def _():
        spectral(t - sch["spec0"])

    @pl.when((t >= sch["spec1"]) & (t < sch["inv1"]))
    def _():
        spectral(t - sch["spec1"])

    @pl.when((t == sch["inv0"]) | (t == sch["inv1"]))
    def _():
        inverse_stage()

    @pl.when((t >= sch["mid"]) & (t < sch["fwd1"]))
    def _():
        tile = t - sch["mid"]
        for b in range(2):
            x1 = short_conv(tile, b)
            for j in range(groups):
                for s in range(n_slab):
                    rows = time_rows(b, tile, j)
                    z = x1[j * NB:(j + 1) * NB, lanes(s)] * (y_buf[s, rows, :] + fb_ref[:, lanes(s)] * v_buf[s, rows, :])
                    v_buf[s, rows, :] = z

    @pl.when(t >= sch["last"])
    def _():
        tile = t - sch["last"]
        for b in range(2):
            x2 = short_conv(tile, b) * _silu(gh_ref[b].astype(F32))
            for j in range(groups):
                for s in range(n_slab):
                    rows = time_rows(b, tile, j)
                    z = x2[j * NB:(j + 1) * NB, lanes(s)] * (y_buf[s, rows, :] + fb_ref[:, lanes(s)] * v_buf[s, rows, :])
                    o_ref[b, j * NB:(j + 1) * NB, lanes(s)] = z.astype(o_ref.dtype)


def _hyena(p, conv_w, conv_b, filt_bias, hf, c):
    bsz, seq, ncols = p.shape
    f1, _, f4, g, gi = _dft_consts(seq)
    sch = _hyena_schedule(seq)
    nt, nk = sch["nt"], sch["nk"]
    n_pairs, na_full = bsz // 2, 2 * seq // NB
    groups_per_c = c // HY_LANES
    halo_per_tile = HY_ROWS // SUBLANES
    pv = p.reshape(n_pairs, 2, seq, ncols)

    def tile_of(t):
        return jnp.where(t < sch["mid"], jnp.minimum(t, nt - 1),
                         jnp.where(t < sch["last"], jnp.clip(t - sch["mid"], 0, nt - 1), jnp.clip(t - sch["last"], 0, nt - 1)))

    def col_of(t, lg):
        return jnp.where(t < sch["mid"], 0, jnp.where(t < sch["last"], 1, 2)) * groups_per_c + lg

    def freq_of(t):
        return jnp.where(t < sch["fwd1"], jnp.clip(t - sch["spec0"], 0, nk - 1), jnp.clip(t - sch["spec1"], 0, nk - 1))

    order_of = lambda t: jnp.where(t < sch["fwd1"], 0, 1)
    last_tile = lambda t: jnp.clip(t - sch["last"], 0, nt - 1)
    const = lambda a: pl.BlockSpec(a.shape, lambda bp, lg, t: (0,) * a.ndim)
    width = 3 * c
    out = pl.pallas_call(
        functools.partial(_hyena_kernel, seq=seq),
        grid=(n_pairs, groups_per_c, sch["total"]),
        in_specs=[
            pl.BlockSpec((None, 2, HY_ROWS, HY_LANES), lambda bp, lg, t: (bp, 0, tile_of(t), col_of(t, lg))),
            pl.BlockSpec((None, 2, SUBLANES, HY_LANES),
                         lambda bp, lg, t: (bp, 0, jnp.maximum(tile_of(t) * halo_per_tile - 1, 0), col_of(t, lg))),
            pl.BlockSpec((None, 2, SUBLANES, HY_LANES),
                         lambda bp, lg, t: (bp, 0, jnp.minimum((tile_of(t) + 1) * halo_per_tile, seq // SUBLANES - 1),
                                            col_of(t, lg))),
            pl.BlockSpec((None, 2, HY_ROWS, HY_LANES), lambda bp, lg, t: (bp, 0, last_tile(t), 3 * groups_per_c + lg)),
            pl.BlockSpec((3, HY_LANES), lambda bp, lg, t: (0, col_of(t, lg))),
            pl.BlockSpec((1, HY_LANES), lambda bp, lg, t: (0, col_of(t, lg))),
            pl.BlockSpec((None, 1, HY_LANES), lambda bp, lg, t: (order_of(t), 0, lg)),
            pl.BlockSpec((None, HY_KA, 2 * NB, HY_LANES), lambda bp, lg, t: (order_of(t), freq_of(t), 0, lg)),
            pl.BlockSpec((HY_KA, 2 * NB, 2 * NB), lambda bp, lg, t: (freq_of(t), 0, 0)),
            pl.BlockSpec((HY_KA, 2 * NB, 2 * NB), lambda bp, lg, t: (freq_of(t), 0, 0)),
            const(f1), const(f4),
        ],
        out_specs=pl.BlockSpec((None, 2, HY_ROWS, HY_LANES), lambda bp, lg, t: (bp, 0, last_tile(t), lg)),
        out_shape=jax.ShapeDtypeStruct((n_pairs, 2, seq, c), BF16),
        scratch_shapes=[
            pltpu.VMEM((HY_LANES // LANES, na_full * PITCH_T, LANES), F32),
            pltpu.VMEM((HY_LANES // LANES, na_full * PITCH_T, LANES), F32),
            pltpu.VMEM((HY_LANES // LANES, na_full * PITCH_F, LANES), F32),
        ],
        compiler_params=pltpu.CompilerParams(dimension_semantics=("parallel", "parallel", "arbitrary")),
        name="hyena_long_conv",
    )(pv, pv, pv, pv, conv_w, conv_b.reshape(1, width), filt_bias.reshape(HYENA_ORDER, 1, c), hf, g, gi, f1, f4)
    return out.reshape(bsz, seq, c)


def _out_kernel(o0_ref, o1_ref, o2_ref, l0_ref, l1_ref, l2_ref, ga_ref, yh_ref, ra_ref, rh_ref, x_ref, mod_ref,
                wpa_ref, wph_ref, wo_ref, bo_ref, lg_ref, lb_ref, out_ref, *, alpha):
    lses = [l0_ref[...], l1_ref[...], l2_ref[...]]
    m = jnp.maximum(jnp.maximum(lses[0], lses[1]), lses[2])
    es = [jnp.exp(l - m) for l in lses]
    o = es[0] * o0_ref[...].astype(F32) + es[1] * o1_ref[...].astype(F32) + es[2] * o2_ref[...].astype(F32)
    o = o / (es[0] + es[1] + es[2])
    ya = (o * _silu(ga_ref[...].astype(F32))).astype(BF16)
    merged = (jax.nn.sigmoid(ra_ref[...].astype(F32)) * _mm(ya, wpa_ref[...])
              + jax.nn.sigmoid(rh_ref[...].astype(F32)) * _mm(yh_ref[...], wph_ref[...]))
    out = _mm(merged.astype(BF16), wo_ref[...]) + bo_ref[...]
    res = alpha * x_ref[...] + mod_ref[2:3, :] * out
    out_ref[...] = _layer_norm(res) * lg_ref[...] + lb_ref[...]


def _out_proj(os_, lses, p, cols, yh, x, mod, wpa, wph, wo, bo, lg, lb, alpha):
    bsz, seq, d = x.shape
    da = wpa.shape[0]
    tm = OUT_TILE
    rows = lambda width, cb=0: pl.BlockSpec((None, tm, width), lambda bi, i: (bi, i, cb))
    const = lambda a: pl.BlockSpec(a.shape, lambda bi, i: (0,) * a.ndim)
    ga_col, ra_col, rh_col = cols
    return pl.pallas_call(
        functools.partial(_out_kernel, alpha=alpha),
        grid=(bsz, seq // tm),
        in_specs=[rows(da)] * 6
        + [rows(da, ga_col // da), rows(da), rows(d, ra_col // d), rows(d, rh_col // d), rows(d),
           pl.BlockSpec((None, 3, d), lambda bi, i: (bi, 0, 0)),
           const(wpa), const(wph), const(wo)]
        + [pl.BlockSpec((1, d), lambda bi, i: (0, 0))] * 3,
        out_specs=rows(d),
        out_shape=jax.ShapeDtypeStruct((bsz, seq, d), F32),
        name="merge_out_proj",
    )(*os_, *lses, p, yh, p, p, x, mod, wpa, wph, wo, bo.reshape(1, d), lg.reshape(1, d), lb.reshape(1, d))


def _layer(x, mod, w_in, b_in, conv_w, conv_b, fw1, fb1, fw2, fb2, fw3, fb3, fw4, ffreq, fbias,
           w_proj_attn, w_proj_hyena, w_out, b_out, ln_g, ln_b, alpha):
    bsz, seq, d = x.shape
    da = w_proj_attn.shape[0]
    c = w_proj_hyena.shape[0]
    n_heads = da // HEAD_DIM

    def qkv_cols(a, g):
        return [a[..., (i * N_GROUPS + g) * da:(i * N_GROUPS + g + 1) * da] for i in range(3)]

    i1 = 3 * N_GROUPS * da
    i2 = i1 + da
    i4 = i2 + 4 * c

    def regroup(a):
        parts = [a[..., i2:i4]] + qkv_cols(a, 0) + [a[..., i1:i2], a[..., i4:]]
        for g in range(1, N_GROUPS):
            parts += qkv_cols(a, g)
        return jnp.concatenate(parts, axis=-1)

    col_qkv0 = 4 * c
    col_ga = col_qkv0 + 3 * da
    col_ra = col_ga + da
    col_rh = col_ra + d
    col_qkv = [col_qkv0] + [col_rh + d + (g - 1) * 3 * da for g in range(1, N_GROUPS)]

    p = _inproj(x, mod, regroup(w_in).astype(BF16), regroup(b_in).reshape(1, -1))

    outs, lses = [], []
    for g, (window, dilation) in enumerate(DILATED_GROUPS):
        o_g, lse_g = _attention(p, col_qkv[g], dilation, window // (2 * dilation), n_heads)
        outs.append(o_g)
        lses.append(lse_g)

    filt = _filters(seq, c, fw1, fb1, fw2, fb2, fw3, fb3, fw4, ffreq)
    hf = _filter_spectra(filt, seq, c)
    yh = _hyena(p, conv_w, conv_b, fbias, hf, c)

    return _out_proj(outs, lses, p, (col_ga, col_ra, col_rh), yh, x, mod,
                     w_proj_attn.astype(BF16), w_proj_hyena.astype(BF16), w_out.astype(BF16),
                     b_out, ln_g, ln_b, alpha)


def kernel(x, c, w_ada, b_ada, w_in, b_in, conv_w, conv_b, filt_w1, filt_b1, filt_w2, filt_b2, filt_w3, filt_b3,
           filt_w4, filt_freq, filt_bias, w_proj_attn, w_proj_hyena, w_out, b_out, ln_g, ln_b):
    depth = w_ada.shape[0]
    alpha = (2 * depth) ** 0.25
    d = x.shape[-1]
    mods = _ada(c, w_ada, b_ada).reshape(depth, c.shape[0], 3, d)
    for l in range(depth):
        x = _layer(x, mods[l], w_in[l], b_in[l], conv_w[l], conv_b[l], filt_w1[l], filt_b1[l], filt_w2[l],
                   filt_b2[l], filt_w3[l], filt_b3[l], filt_w4[l], filt_freq[l], filt_bias[l],
                   w_proj_attn[l], w_proj_hyena[l], w_out[l], b_out[l], ln_g[l], ln_b[l], alpha)
    return x
```

```python
import functools
import math

import numpy as np
import jax
import jax.numpy as jnp
from jax import lax
from jax.experimental import pallas as pl
from jax.experimental.pallas import tpu as pltpu

F32 = jnp.float32
BF16 = jnp.bfloat16

HEAD_DIM = 64
DILATED_GROUPS = ((128, 1), (512, 4), (2048, 16))
N_GROUPS = len(DILATED_GROUPS)
HYENA_ORDER = 2
FILTER_EMB = 33
FILTER_WIDTH = 64
DECAY_TARGET = 1e-2
FAST_DECAY_PCT = 0.3
SLOW_DECAY_PCT = 1.5
LN_EPS = 1e-5

LANES = 128
NB = 64
PAD_EMB = 128
NEG_BIG = -1e30

ROW_TILE = 1024
COL_TILE = 1536
OUT_TILE = 512
ATTN_UNROLL = 4
ATTN_STRIDE = 4
NB_STEP = 8
KA_STEP = 8
HY_LANES = 256
HY_ROWS = 1024
HY_KA = 32
HY_UNROLL = 16
SUBLANES = 8
PITCH_T = NB + SUBLANES
PITCH_F = 2 * NB + SUBLANES


def _mm(a, b):
    return jnp.dot(a, b, preferred_element_type=F32)


def _split(a):
    hi = a.astype(BF16)
    lo = (a - hi.astype(F32)).astype(BF16)
    return hi, lo


def _dot3(a, b):
    ah, al = _split(a)
    bh, bl = _split(b)
    return _mm(ah, bh) + _mm(ah, bl) + _mm(al, bh)


def _layer_norm(x):
    mu = jnp.mean(x, axis=-1, keepdims=True)
    xc = x - mu
    var = jnp.mean(xc * xc, axis=-1, keepdims=True)
    return xc * lax.rsqrt(var + LN_EPS)


def _silu(a):
    return a * jax.nn.sigmoid(a)


@functools.lru_cache(maxsize=None)
def _dft_consts(seq):
    n = 2 * seq
    na_full = n // NB
    na_half = na_full // 2
    ka = np.arange(na_full)[:, None]
    na = np.arange(na_half)[None, :]
    ang = 2.0 * np.pi * ((ka * na) % na_full) / na_full
    c, s = np.cos(ang), np.sin(ang)
    f1 = np.block([[c, s], [-s, c]])
    f1r = np.concatenate([c, -s], axis=0)
    f4 = np.block([[c.T, -s.T], [s.T, c.T]]) / n
    kav = np.arange(na_full)[:, None, None]
    kb = np.arange(NB)[None, :, None]
    nb = np.arange(NB)[None, None, :]
    th = 2.0 * np.pi * (((nb * kav) % n) / n + ((nb * kb) % NB) / NB)
    cg, sg = np.cos(th), np.sin(th)
    g = np.concatenate([np.concatenate([cg, sg], axis=2), np.concatenate([-sg, cg], axis=2)], axis=1)
    cgt, sgt = cg.transpose(0, 2, 1), sg.transpose(0, 2, 1)
    gi = np.concatenate([np.concatenate([cgt, -sgt], axis=2), np.concatenate([sgt, cgt], axis=2)], axis=1)
    as_bf16 = lambda a: jnp.asarray(a, dtype=F32).astype(BF16)
    return as_bf16(f1), as_bf16(f1r), as_bf16(f4), as_bf16(g), as_bf16(gi)


@functools.lru_cache(maxsize=None)
def _filter_consts(seq, channels):
    t = np.linspace(0.0, 1.0, seq)[:, None]
    bands = (FILTER_EMB - 1) // 2
    w = 2.0 * np.pi * np.arange(seq)[:, None] / seq
    f = np.linspace(1e-4, bands - 1, bands)[None, :]
    pos = np.concatenate([t, np.cos(f * w), -np.sin(f * w)], axis=-1)
    pos = np.pad(pos, ((0, 0), (0, PAD_EMB - FILTER_EMB)))
    deltas = np.linspace(math.log(DECAY_TARGET) / SLOW_DECAY_PCT, math.log(DECAY_TARGET) / FAST_DECAY_PCT, channels)
    decay = np.exp(-t * np.abs(deltas)[None, :])
    return jnp.asarray(pos, dtype=F32), jnp.asarray(decay, dtype=F32)


@functools.lru_cache(maxsize=None)
def _attn_bias(n_heads, dilation, radius):
    tq, tk = 2 * radius, 4 * radius
    slopes = 2.0 ** (-8.0 * (np.arange(n_heads) + 1) / n_heads)
    i = np.arange(tq)[:, None]
    j = np.arange(tk)[None, :]
    cases = []
    for off in (0, radius, 2 * radius):
        rel = np.abs(j - off - i)
        per_head = [np.where(rel <= radius, -slopes[h] * rel * dilation, NEG_BIG) for h in range(n_heads)]
        cases.append(np.stack([np.concatenate(per_head[2 * p:2 * p + 2], axis=0) for p in range(n_heads // 2)]))
    return jnp.asarray(np.stack(cases), dtype=F32)


def _ada_kernel(c_ref, w_ref, b_ref, o_ref):
    o_ref[...] = _dot3(c_ref[...], w_ref[...]) + b_ref[...]


def _ada(c, w_ada, b_ada):
    depth, d, n3 = w_ada.shape
    bsz = c.shape[0]
    tn = d
    return pl.pallas_call(
        _ada_kernel,
        grid=(depth, n3 // tn),
        in_specs=[
            pl.BlockSpec((bsz, d), lambda l, j: (0, 0)),
            pl.BlockSpec((None, d, tn), lambda l, j: (l, 0, j)),
            pl.BlockSpec((None, 1, tn), lambda l, j: (l, 0, j)),
        ],
        out_specs=pl.BlockSpec((None, bsz, tn), lambda l, j: (l, 0, j)),
        out_shape=jax.ShapeDtypeStruct((depth, bsz, n3), F32),
        name="ada_mod",
    )(c, w_ada, b_ada.reshape(depth, 1, n3))


def _inproj_kernel(x_ref, mod_ref, w_ref, b_ref, o_ref, h_ref):
    @pl.when(pl.program_id(2) == 0)
    def _():
        hn = _layer_norm(x_ref[...])
        shift, scale = mod_ref[0:1, :], mod_ref[1:2, :]
        h_ref[...] = (hn * (1.0 + scale) + shift).astype(BF16)

    o_ref[...] = (_mm(h_ref[...], w_ref[...]) + b_ref[...]).astype(o_ref.dtype)


def _inproj(x, mod, w, b):
    bsz, seq, d = x.shape
    n = w.shape[1]
    tm = ROW_TILE
    col = lambda rows: pl.BlockSpec((rows, COL_TILE), lambda bi, i, j: (0, j))
    return pl.pallas_call(
        _inproj_kernel,
        grid=(bsz, seq // tm, n // COL_TILE),
        in_specs=[
            pl.BlockSpec((None, tm, d), lambda bi, i, j: (bi, i, 0)),
            pl.BlockSpec((None, 3, d), lambda bi, i, j: (bi, 0, 0)),
            col(d), col(1),
        ],
        out_specs=pl.BlockSpec((None, tm, COL_TILE), lambda bi, i, j: (bi, i, j)),
        out_shape=jax.ShapeDtypeStruct((bsz, seq, n), BF16),
        scratch_shapes=[pltpu.VMEM((tm, d), BF16)],
        compiler_params=pltpu.CompilerParams(dimension_semantics=("parallel", "parallel", "arbitrary")),
        name="inproj",
    )(x, mod, w, b)


def _attn_kernel(q_ref, k_ref, v_ref, bias_ref, o_ref, lse_ref, *scratch, dilation, seq, radius):
    tq, tk = 2 * radius, 4 * radius
    d = dilation
    length = seq // d
    nblk = length // tq
    first_head = lax.broadcasted_iota(jnp.int32, (tq, LANES), 1) < HEAD_DIM
    pre = max(d // ATTN_STRIDE, 1)
    inner, sub = d // pre, seq // pre
    if d > 1:
        qf, kf, vf, of, lf = scratch[:5]
        for ref, buf in ((q_ref, qf), (k_ref, kf), (v_ref, vf)):
            if pre == 1:
                buf[...] = ref[...].astype(F32)
            else:
                tmp = scratch[5]
                tmp[...] = ref[...].astype(F32)
                for r1 in range(pre):
                    buf[pl.ds(r1 * sub, sub), :] = tmp[pl.ds(r1, sub, stride=pre), :]

    def rows(r, start, n):
        if d == 1:
            return pl.ds(pl.multiple_of(start, radius), n)
        r1, r2 = r & (pre - 1), lax.shift_right_logical(r, pre.bit_length() - 1)
        return pl.ds(r1 * sub + r2 + inner * start, n, stride=inner)

    def block(i, carry):
        r, jb = lax.shift_right_logical(i, nblk.bit_length() - 1), i & (nblk - 1)
        q0 = jb * tq
        ks = jnp.clip(q0 - radius, 0, length - tk)
        case = jnp.where(jb == 0, 0, jnp.where(jb == nblk - 1, 2, 1))
        if d == 1:
            q2, k2, v2 = q_ref[rows(r, q0, tq), :], k_ref[rows(r, ks, tk), :], v_ref[rows(r, ks, tk), :]
        else:
            q2, k2, v2 = (ref[rows(r, st, n), :].astype(BF16) for ref, st, n in ((qf, q0, tq), (kf, ks, tk), (vf, ks, tk)))
        q2 = q2 * (HEAD_DIM ** -0.5)
        zero = jnp.zeros_like(q2)
        qs = jnp.concatenate([jnp.where(first_head, q2, zero), jnp.where(first_head, zero, q2)], axis=0)
        s = lax.dot_general(qs, k2, (((1,), (1,)), ((), ())), preferred_element_type=F32) + bias_ref[case]
        m = jnp.max(s, axis=1, keepdims=True)
        e = jnp.exp(s - m).astype(BF16)
        res = _mm(e, jnp.concatenate([v2, jnp.ones_like(v2)], axis=1))
        den = jnp.where(first_head, res[:tq, LANES:], res[tq:, LANES:])
        o = jnp.where(first_head, res[:tq, :LANES], res[tq:, :LANES]) / den
        lse = jnp.where(first_head, m[:tq], m[tq:]) + jnp.log(den)
        if d == 1:
            o_ref[rows(r, q0, tq), :] = o.astype(o_ref.dtype)
            lse_ref[rows(r, q0, tq), :] = lse
        else:
            of[rows(r, q0, tq), :] = o
            lf[rows(r, q0, tq), :] = lse
        return carry

    lax.fori_loop(0, d * nblk, block, 0, unroll=ATTN_UNROLL)
    if d > 1 and pre == 1:
        o_ref[...] = of[...].astype(o_ref.dtype)
        lse_ref[...] = lf[...]
    elif d > 1:
        tmp = scratch[5]
        for r1 in range(pre):
            tmp[pl.ds(r1, sub, stride=pre), :] = of[pl.ds(r1 * sub, sub), :]
            lse_ref[pl.ds(r1, sub, stride=pre), :] = lf[pl.ds(r1 * sub, sub), :]
        o_ref[...] = tmp[...].astype(o_ref.dtype)


def _attention(p, col0, dilation, radius, n_heads):
    bsz, seq, _ = p.shape
    da = n_heads * HEAD_DIM
    n_pairs = da // LANES
    bias = _attn_bias(n_heads, dilation, radius)
    kern = functools.partial(_attn_kernel, dilation=dilation, seq=seq, radius=radius)
    col = lambda k: (lambda bi, pr: (bi, 0, (col0 + k * da) // LANES + pr))
    return pl.pallas_call(
        kern,
        grid=(bsz, n_pairs),
        in_specs=[pl.BlockSpec((None, seq, LANES), col(k)) for k in range(3)]
        + [pl.BlockSpec((3, None) + bias.shape[2:], lambda bi, pr: (0, pr, 0, 0))],
        out_specs=[pl.BlockSpec((None, seq, LANES), lambda bi, pr: (bi, 0, pr))] * 2,
        out_shape=[jax.ShapeDtypeStruct((bsz, seq, da), BF16), jax.ShapeDtypeStruct((bsz, seq, da), F32)],
        scratch_shapes=[pltpu.VMEM((seq, LANES), F32)] * (0 if dilation == 1 else 5 if dilation <= ATTN_STRIDE else 6),
        name=f"attn_d{dilation}",
    )(p, p, p, bias)


def _filter_kernel(pos_ref, w1_ref, b1_ref, w2_ref, b2_ref, w3_ref, b3_ref, w4_ref, fr_ref, dec_ref, o_ref):
    fr = fr_ref[...]
    h = jnp.sin(fr * (_dot3(pos_ref[...], w1_ref[...]) + b1_ref[...]))
    h = jnp.sin(fr * (_dot3(h, w2_ref[...]) + b2_ref[...]))
    h = jnp.sin(fr * (_dot3(h, w3_ref[...]) + b3_ref[...]))
    f = _dot3(h, w4_ref[...])
    dec = dec_ref[...]
    c = dec.shape[1]
    row = pl.program_id(0) * pos_ref.shape[0] + lax.broadcasted_iota(jnp.int32, (pos_ref.shape[0], 1), 0)
    for o in range(HYENA_ORDER):
        fwd = f[:, (2 * o) * c:(2 * o + 1) * c] * dec
        bwd = jnp.where(row == 0, 0.0, f[:, (2 * o + 1) * c:(2 * o + 2) * c] * dec)
        o_ref[:, (2 * o) * c:(2 * o + 1) * c] = fwd + bwd
        o_ref[:, (2 * o + 1) * c:(2 * o + 2) * c] = bwd - fwd


def _filters(seq, c, w1, b1, w2, b2, w3, b3, w4, freq):
    pos, decay = _filter_consts(seq, c)
    tm = 512
    w1p = jnp.pad(w1, ((0, PAD_EMB - FILTER_EMB), (0, 0)))
    full = lambda a: pl.BlockSpec(a.shape, lambda i: (0,) * a.ndim)
    args = (w1p, b1.reshape(1, -1), w2, b2.reshape(1, -1), w3, b3.reshape(1, -1), w4, freq.reshape(1, -1))
    return pl.pallas_call(
        _filter_kernel,
        grid=(seq // tm,),
        in_specs=[pl.BlockSpec((tm, PAD_EMB), lambda i: (i, 0))] + [full(a) for a in args]
        + [pl.BlockSpec((tm, c), lambda i: (i, 0))],
        out_specs=pl.BlockSpec((tm, 2 * HYENA_ORDER * c), lambda i: (i, 0)),
        out_shape=jax.ShapeDtypeStruct((seq, 2 * HYENA_ORDER * c), F32),
        name="hyena_filters",
    )(pos, *args, decay)


def _store_spectrum(y_ref, l, y):
    half, c = y.shape[0] // 2, y.shape[1]
    y_ref[l, :, 0:c] = y[:half].astype(y_ref.dtype)
    y_ref[l, :, c:2 * c] = y[half:].astype(y_ref.dtype)


def _load_spectrum(blk):
    c = blk.shape[1] // 2
    return jnp.concatenate([blk[:, :c], blk[:, c:]], axis=0)


def _fstage1_kernel(f_ref, f1r_ref, y_ref):
    for l in range(f_ref.shape[0]):
        _store_spectrum(y_ref, l, _mm(f1r_ref[...], f_ref[l].astype(BF16)))


def _fstage2_kernel(e_ref, d_ref, g_ref, h_ref):
    for l in range(e_ref.shape[0]):
        h_ref[l, 0:NB, :] = _mm(g_ref[l, 0:NB, :], _load_spectrum(e_ref[l])).astype(h_ref.dtype)
        h_ref[l, NB:2 * NB, :] = (-_mm(g_ref[l, NB:2 * NB, :], _load_spectrum(d_ref[l]))).astype(h_ref.dtype)


def _filter_spectra(filt, seq, c):
    _, f1r, _, g, _ = _dft_consts(seq)
    na_full = 2 * seq // NB
    na_half = na_full // 2
    n_cb = 2 * HYENA_ORDER
    fv = filt.reshape(na_half, NB, n_cb * c).transpose(1, 0, 2)
    yf = pl.pallas_call(
        _fstage1_kernel,
        grid=(n_cb, NB // NB_STEP),
        in_specs=[pl.BlockSpec((NB_STEP, na_half, c), lambda cb, s: (s, 0, cb)),
                  pl.BlockSpec(f1r.shape, lambda cb, s: (0, 0))],
        out_specs=pl.BlockSpec((None, NB_STEP, na_full, 2 * c), lambda cb, s: (cb, s, 0, 0)),
        out_shape=jax.ShapeDtypeStruct((n_cb, NB, na_full, 2 * c), BF16),
        name="hyena_filter_stage1",
    )(fv, f1r)
    yf = yf.transpose(0, 2, 1, 3)
    return pl.pallas_call(
        _fstage2_kernel,
        grid=(HYENA_ORDER, na_full // KA_STEP),
        in_specs=[pl.BlockSpec((None, KA_STEP, NB, 2 * c), lambda o, s: (2 * o, s, 0, 0)),
                  pl.BlockSpec((None, KA_STEP, NB, 2 * c), lambda o, s: (2 * o + 1, s, 0, 0)),
                  pl.BlockSpec((KA_STEP, 2 * NB, 2 * NB), lambda o, s: (s, 0, 0))],
        out_specs=pl.BlockSpec((None, KA_STEP, 2 * NB, c), lambda o, s: (o, s, 0, 0)),
        out_shape=jax.ShapeDtypeStruct((HYENA_ORDER, na_full, 2 * NB, c), BF16),
        name="hyena_filter_stage2",
    )(yf, yf, g)


def _hyena_schedule(seq):
    nt, nk = seq // HY_ROWS, (2 * seq // NB) // HY_KA
    s = {"load": 0, "fwd0": nt, "spec0": nt + 1, "inv0": nt + 1 + nk, "mid": nt + 2 + nk}
    s.update(fwd1=s["mid"] + nt, spec1=s["mid"] + nt + 1, inv1=s["mid"] + nt + 1 + nk, last=s["mid"] + nt + 2 + nk)
    s.update(total=s["last"] + nt, nt=nt, nk=nk)
    return s


def _hyena_kernel(p_ref, prev_ref, next_ref, gh_ref, cw_ref, cb_ref, fb_ref, hf_ref, g_ref, gi_ref, f1_ref, f4_ref,
                  o_ref, v_buf, y_buf, s_buf, *, seq):
    sch = _hyena_schedule(seq)
    t = pl.program_id(2)
    n_slab = HY_LANES // LANES
    na_half = seq // NB
    na_full, pair_rows = 2 * na_half, 2 * na_half
    groups = HY_ROWS // NB
    lanes = lambda s: slice(s * LANES, (s + 1) * LANES)
    row = lax.broadcasted_iota(jnp.int32, (HY_ROWS, 1), 0)

    def short_conv(tile, b):
        a = p_ref[b].astype(F32)
        top = jnp.where(tile > 0, prev_ref[b, SUBLANES - 1:SUBLANES, :].astype(F32), 0.0)
        bot = jnp.where(tile < sch["nt"] - 1, next_ref[b, 0:1, :].astype(F32), 0.0)
        up = jnp.where(row == 0, top, pltpu.roll(a, 1, 0))
        dn = jnp.where(row == HY_ROWS - 1, bot, pltpu.roll(a, HY_ROWS - 1, 0))
        return cw_ref[0:1, :] * up + cw_ref[1:2, :] * a + cw_ref[2:3, :] * dn + cb_ref[...]

    def time_rows(b, tile, j):
        return pl.ds(pl.multiple_of((b * na_half + tile * groups + j) * PITCH_T, SUBLANES), NB)

    def forward_stage1():
        def body(nb, carry):
            rhs = jnp.concatenate([v_buf[s, pl.ds(nb, pair_rows, stride=PITCH_T), :] for s in range(n_slab)], axis=1)
            y = _mm(f1_ref[...], rhs.astype(BF16))
            for ri in range(2):
                for s in range(n_slab):
                    s_buf[s, pl.ds(ri * NB + nb, na_full, stride=PITCH_F), :] = y[ri * na_full:(ri + 1) * na_full, lanes(s)]
            return carry
        lax.fori_loop(0, NB, body, 0, unroll=HY_UNROLL)

    def spectral(step):
        def body(l, carry):
            rows = pl.ds(pl.multiple_of((step * HY_KA + l) * PITCH_F, SUBLANES), 2 * NB)
            rhs = jnp.concatenate([s_buf[s, rows, :] for s in range(n_slab)], axis=1)
            x = _mm(g_ref[l], rhs.astype(BF16))
            xr, xi = x[:NB], x[NB:]
            hr, hi = hf_ref[l, 0:NB, :].astype(F32), hf_ref[l, NB:2 * NB, :].astype(F32)
            prod = jnp.concatenate([xr * hr - xi * hi, xr * hi + xi * hr], axis=0)
            tt = _mm(gi_ref[l], prod.astype(BF16))
            for s in range(n_slab):
                s_buf[s, rows, :] = tt[:, lanes(s)]
            return carry
        lax.fori_loop(0, HY_KA, body, 0, unroll=HY_UNROLL)

    def inverse_stage():
        def body(nb, carry):
            parts = [jnp.concatenate([s_buf[s, pl.ds(ri * NB + nb, na_full, stride=PITCH_F), :] for s in range(n_slab)],
                                     axis=1) for ri in range(2)]
            y = _mm(f4_ref[...], jnp.concatenate(parts, axis=0).astype(BF16))
            for s in range(n_slab):
                y_buf[s, pl.ds(nb, pair_rows, stride=PITCH_T), :] = y[:, lanes(s)]
            return carry
        lax.fori_loop(0, NB, body, 0, unroll=HY_UNROLL)

    @pl.when(t < sch["fwd0"])
    def _():
        for b in range(2):
            v = short_conv(t, b)
            for j in range(groups):
                for s in range(n_slab):
                    v_buf[s, time_rows(b, t, j), :] = v[j * NB:(j + 1) * NB, lanes(s)]

    @pl.when((t == sch["fwd0"]) | (t == sch["fwd1"]))
    def _():
        forward_stage1()

    @pl.when((t >= sch["spec0"]) & (t < sch["inv0"]))
    def _():
        spectral(t - sch["spec0"])

    @pl.when((t >= sch["spec1"]) & (t < sch["inv1"]))
    def _():
        spectral(t - sch["spec1"])

    @pl.when((t == sch["inv0"]) | (t == sch["inv1"]))
    def _():
        inverse_stage()

    @pl.when((t >= sch["mid"]) & (t < sch["fwd1"]))
    def _():
        tile = t - sch["mid"]
        for b in range(2):
            x1 = short_conv(tile, b)
            for j in range(groups):
                for s in range(n_slab):
                    rows = time_rows(b, tile, j)
                    z = x1[j * NB:(j + 1) * NB, lanes(s)] * (y_buf[s, rows, :] + fb_ref[:, lanes(s)] * v_buf[s, rows, :])
                    v_buf[s, rows, :] = z

    @pl.when(t >= sch["last"])
    def _():
        tile = t - sch["last"]
        for b in range(2):
            x2 = short_conv(tile, b) * _silu(gh_ref[b].astype(F32))
            for j in range(groups):
                for s in range(n_slab):
                    rows = time_rows(b, tile, j)
                    z = x2[j * NB:(j + 1) * NB, lanes(s)] * (y_buf[s, rows, :] + fb_ref[:, lanes(s)] * v_buf[s, rows, :])
                    o_ref[b, j * NB:(j + 1) * NB, lanes(s)] = z.astype(o_ref.dtype)


def _hyena(p, col_gate, conv_w, conv_b, filt_bias, hf, c):
    bsz, seq, ncols = p.shape
    f1, _, f4, g, gi = _dft_consts(seq)
    sch = _hyena_schedule(seq)
    nt, nk = sch["nt"], sch["nk"]
    n_pairs, na_full = bsz // 2, 2 * seq // NB
    groups_per_c = c // HY_LANES
    halo_per_tile = HY_ROWS // SUBLANES
    pv = p.reshape(n_pairs, 2, seq, ncols)

    def tile_of(t):
        return jnp.where(t < sch["mid"], jnp.minimum(t, nt - 1),
                         jnp.where(t < sch["last"], jnp.clip(t - sch["mid"], 0, nt - 1), jnp.clip(t - sch["last"], 0, nt - 1)))

    def col_of(t, lg):
        return jnp.where(t < sch["mid"], 0, jnp.where(t < sch["last"], 1, 2)) * groups_per_c + lg

    def freq_of(t):
        return jnp.where(t < sch["fwd1"], jnp.clip(t - sch["spec0"], 0, nk - 1), jnp.clip(t - sch["spec1"], 0, nk - 1))

    order_of = lambda t: jnp.where(t < sch["fwd1"], 0, 1)
    last_tile = lambda t: jnp.clip(t - sch["last"], 0, nt - 1)
    const = lambda a: pl.BlockSpec(a.shape, lambda bp, lg, t: (0,) * a.ndim)
    width = 3 * c
    out = pl.pallas_call(
        functools.partial(_hyena_kernel, seq=seq),
        grid=(n_pairs, groups_per_c, sch["total"]),
        in_specs=[
            pl.BlockSpec((None, 2, HY_ROWS, HY_LANES), lambda bp, lg, t: (bp, 0, tile_of(t), col_of(t, lg))),
            pl.BlockSpec((None, 2, SUBLANES, HY_LANES),
                         lambda bp, lg, t: (bp, 0, jnp.maximum(tile_of(t) * halo_per_tile - 1, 0), col_of(t, lg))),
            pl.BlockSpec((None, 2, SUBLANES, HY_LANES),
                         lambda bp, lg, t: (bp, 0, jnp.minimum((tile_of(t) + 1) * halo_per_tile, seq // SUBLANES - 1),
                                            col_of(t, lg))),
            pl.BlockSpec((None, 2, HY_ROWS, HY_LANES), lambda bp, lg, t: (bp, 0, last_tile(t), col_gate // HY_LANES + lg)),
            pl.BlockSpec((3, HY_LANES), lambda bp, lg, t: (0, col_of(t, lg))),
            pl.BlockSpec((1, HY_LANES), lambda bp, lg, t: (0, col_of(t, lg))),
            pl.BlockSpec((None, 1, HY_LANES), lambda bp, lg, t: (order_of(t), 0, lg)),
            pl.BlockSpec((None, HY_KA, 2 * NB, HY_LANES), lambda bp, lg, t: (order_of(t), freq_of(t), 0, lg)),
            pl.BlockSpec((HY_KA, 2 * NB, 2 * NB), lambda bp, lg, t: (freq_of(t), 0, 0)),
            pl.BlockSpec((HY_KA, 2 * NB, 2 * NB), lambda bp, lg, t: (freq_of(t), 0, 0)),
            const(f1), const(f4),
        ],
        out_specs=pl.BlockSpec((None, 2, HY_ROWS, HY_LANES), lambda bp, lg, t: (bp, 0, last_tile(t), lg)),
        out_shape=jax.ShapeDtypeStruct((n_pairs, 2, seq, c), BF16),
        scratch_shapes=[
            pltpu.VMEM((HY_LANES // LANES, na_full * PITCH_T, LANES), F32),
            pltpu.VMEM((HY_LANES // LANES, na_full * PITCH_T, LANES), F32),
            pltpu.VMEM((HY_LANES // LANES, na_full * PITCH_F, LANES), F32),
        ],
        compiler_params=pltpu.CompilerParams(dimension_semantics=("parallel", "parallel", "arbitrary")),
        name="hyena_long_conv",
    )(pv, pv, pv, pv, conv_w, conv_b.reshape(1, width), filt_bias.reshape(HYENA_ORDER, 1, c), hf, g, gi, f1, f4)
    return out.reshape(bsz, seq, c)


def _out_kernel(o0_ref, o1_ref, o2_ref, l0_ref, l1_ref, l2_ref, ga_ref, yh_ref, ra_ref, rh_ref, x_ref, mod_ref,
                wpa_ref, wph_ref, wo_ref, bo_ref, lg_ref, lb_ref, out_ref, *, alpha):
    lses = [l0_ref[...], l1_ref[...], l2_ref[...]]
    m = jnp.maximum(jnp.maximum(lses[0], lses[1]), lses[2])
    es = [jnp.exp(l - m) for l in lses]
    o = es[0] * o0_ref[...].astype(F32) + es[1] * o1_ref[...].astype(F32) + es[2] * o2_ref[...].astype(F32)
    o = o / (es[0] + es[1] + es[2])
    ya = (o * _silu(ga_ref[...].astype(F32))).astype(BF16)
    merged = (jax.nn.sigmoid(ra_ref[...].astype(F32)) * _mm(ya, wpa_ref[...])
              + jax.nn.sigmoid(rh_ref[...].astype(F32)) * _mm(yh_ref[...], wph_ref[...]))
    out = _mm(merged.astype(BF16), wo_ref[...]) + bo_ref[...]
    res = alpha * x_ref[...] + mod_ref[2:3, :] * out
    out_ref[...] = _layer_norm(res) * lg_ref[...] + lb_ref[...]


def _out_proj(os_, lses, p, cols, yh, x, mod, wpa, wph, wo, bo, lg, lb, alpha):
    bsz, seq, d = x.shape
    da = wpa.shape[0]
    tm = OUT_TILE
    rows = lambda width, cb=0: pl.BlockSpec((None, tm, width), lambda bi, i: (bi, i, cb))
    const = lambda a: pl.BlockSpec(a.shape, lambda bi, i: (0,) * a.ndim)
    ga_col, ra_col, rh_col = cols
    return pl.pallas_call(
        functools.partial(_out_kernel, alpha=alpha),
        grid=(bsz, seq // tm),
        in_specs=[rows(da)] * 6
        + [rows(da, ga_col // da), rows(da), rows(d, ra_col // d), rows(d, rh_col // d), rows(d),
           pl.BlockSpec((None, 3, d), lambda bi, i: (bi, 0, 0)),
           const(wpa), const(wph), const(wo)]
        + [pl.BlockSpec((1, d), lambda bi, i: (0, 0))] * 3,
        out_specs=rows(d),
        out_shape=jax.ShapeDtypeStruct((bsz, seq, d), F32),
        name="merge_out_proj",
    )(*os_, *lses, p, yh, p, p, x, mod, wpa, wph, wo, bo.reshape(1, d), lg.reshape(1, d), lb.reshape(1, d))


def _layer(x, mod, w_in, b_in, conv_w, conv_b, fw1, fb1, fw2, fb2, fw3, fb3, fw4, ffreq, fbias,
           w_proj_attn, w_proj_hyena, w_out, b_out, ln_g, ln_b, alpha):
    bsz, seq, d = x.shape
    da = w_proj_attn.shape[0]
    c = w_proj_hyena.shape[0]
    n_heads = da // HEAD_DIM

    def qkv_cols(a, g):
        return [a[..., (i * N_GROUPS + g) * da:(i * N_GROUPS + g + 1) * da] for i in range(3)]

    i1 = 3 * N_GROUPS * da
    i2 = i1 + da
    i3 = i2 + 3 * c
    i4 = i3 + c

    def regroup(a):
        parts = [a[..., i2:i3]] + qkv_cols(a, 0) + [a[..., i3:i4], a[..., i1:i2], a[..., i4:]]
        for g in range(1, N_GROUPS):
            parts += qkv_cols(a, g)
        return jnp.concatenate(parts, axis=-1)

    col_qkv0 = 3 * c
    col_gh = col_qkv0 + 3 * da
    col_ga = col_gh + c
    col_ra = col_ga + da
    col_rh = col_ra + d
    col_qkv = [col_qkv0] + [col_rh + d + (g - 1) * 3 * da for g in range(1, N_GROUPS)]

    p = _inproj(x, mod, regroup(w_in).astype(BF16), regroup(b_in).reshape(1, -1))

    outs, lses = [], []
    for g, (window, dilation) in enumerate(DILATED_GROUPS):
        o_g, lse_g = _attention(p, col_qkv[g], dilation, window // (2 * dilation), n_heads)
        outs.append(o_g)
        lses.append(lse_g)

    filt = _filters(seq, c, fw1, fb1, fw2, fb2, fw3, fb3, fw4, ffreq)
    hf = _filter_spectra(filt, seq, c)
    yh = _hyena(p, col_gh, conv_w, conv_b, fbias, hf, c)

    return _out_proj(outs, lses, p, (col_ga, col_ra, col_rh), yh, x, mod,
                     w_proj_attn.astype(BF16), w_proj_hyena.astype(BF16), w_out.astype(BF16),
                     b_out, ln_g, ln_b, alpha)


def kernel(x, c, w_ada, b_ada, w_in, b_in, conv_w, conv_b, filt_w1, filt_b1, filt_w2, filt_b2, filt_w3, filt_b3,
           filt_w4, filt_freq, filt_bias, w_proj_attn, w_proj_hyena, w_out, b_out, ln_g, ln_b):
    depth = w_ada.shape[0]
    alpha = (2 * depth) ** 0.25
    d = x.shape[-1]
    mods = _ada(c, w_ada, b_ada).reshape(depth, c.shape[0], 3, d)
    for l in range(depth):
        x = _layer(x, mods[l], w_in[l], b_in[l], conv_w[l], conv_b[l], filt_w1[l], filt_b1[l], filt_w2[l],
                   filt_b2[l], filt_w3[l], filt_b3[l], filt_w4[l], filt_freq[l], filt_bias[l],
                   w_proj_attn[l], w_proj_hyena[l], w_out[l], b_out[l], ln_g[l], ln_b[l], alpha)
    return x
```

```python
import functools
import math

import numpy as np
import jax
import jax.numpy as jnp
from jax import lax
from jax.experimental import pallas as pl
from jax.experimental.pallas import tpu as pltpu

F32 = jnp.float32
BF16 = jnp.bfloat16

HEAD_DIM = 64
DILATED_GROUPS = ((128, 1), (512, 4), (2048, 16))
N_GROUPS = len(DILATED_GROUPS)
HYENA_ORDER = 2
FILTER_EMB = 33
FILTER_WIDTH = 64
DECAY_TARGET = 1e-2
FAST_DECAY_PCT = 0.3
SLOW_DECAY_PCT = 1.5
LN_EPS = 1e-5

LANES = 128
NB = 64
PAD_EMB = 128
NEG_BIG = -1e30

ROW_TILE = 2048
COL_TILE = 1536
OUT_TILE = 512
ATTN_UNROLL = 4
ATTN_STRIDE = 4
HY_LANES = 256
HY_ROWS = 1024
HY_KA = 32
HY_UNROLL = 16
SUBLANES = 8
PITCH_T = NB + SUBLANES
PITCH_F = 2 * NB + SUBLANES


def _mm(a, b):
    return jnp.dot(a, b, preferred_element_type=F32)


def _split(a):
    hi = a.astype(BF16)
    lo = (a - hi.astype(F32)).astype(BF16)
    return hi, lo


def _dot3(a, b):
    ah, al = _split(a)
    bh, bl = _split(b)
    return _mm(ah, bh) + _mm(ah, bl) + _mm(al, bh)


def _layer_norm(x):
    mu = jnp.mean(x, axis=-1, keepdims=True)
    xc = x - mu
    var = jnp.mean(xc * xc, axis=-1, keepdims=True)
    return xc * lax.rsqrt(var + LN_EPS)


def _silu(a):
    return a * jax.nn.sigmoid(a)


@functools.lru_cache(maxsize=None)
def _dft_consts(seq):
    n = 2 * seq
    na_full = n // NB
    na_half = na_full // 2
    ka = np.arange(na_full)[:, None]
    na = np.arange(na_half)[None, :]
    ang = 2.0 * np.pi * ((ka * na) % na_full) / na_full
    c, s = np.cos(ang), np.sin(ang)
    f1 = np.block([[c, s], [-s, c]])
    f1r = np.concatenate([c, -s], axis=0)
    f4 = np.block([[c.T, -s.T], [s.T, c.T]]) / n
    kav = np.arange(na_full)[:, None, None]
    kb = np.arange(NB)[None, :, None]
    nb = np.arange(NB)[None, None, :]
    th = 2.0 * np.pi * (((nb * kav) % n) / n + ((nb * kb) % NB) / NB)
    cg, sg = np.cos(th), np.sin(th)
    g = np.concatenate([np.concatenate([cg, sg], axis=2), np.concatenate([-sg, cg], axis=2)], axis=1)
    cgt, sgt = cg.transpose(0, 2, 1), sg.transpose(0, 2, 1)
    gi = np.concatenate([np.concatenate([cgt, -sgt], axis=2), np.concatenate([sgt, cgt], axis=2)], axis=1)
    as_bf16 = lambda a: jnp.asarray(a, dtype=F32).astype(BF16)
    return as_bf16(f1), as_bf16(f1r), as_bf16(f4), as_bf16(g), as_bf16(gi)


@functools.lru_cache(maxsize=None)
def _filter_consts(seq, channels):
    t = np.linspace(0.0, 1.0, seq)[:, None]
    bands = (FILTER_EMB - 1) // 2
    w = 2.0 * np.pi * np.arange(seq)[:, None] / seq
    f = np.linspace(1e-4, bands - 1, bands)[None, :]
    pos = np.concatenate([t, np.cos(f * w), -np.sin(f * w)], axis=-1)
    pos = np.pad(pos, ((0, 0), (0, PAD_EMB - FILTER_EMB)))
    deltas = np.linspace(math.log(DECAY_TARGET) / SLOW_DECAY_PCT, math.log(DECAY_TARGET) / FAST_DECAY_PCT, channels)
    decay = np.exp(-t * np.abs(deltas)[None, :])
    return jnp.asarray(pos, dtype=F32), jnp.asarray(decay, dtype=F32)


@functools.lru_cache(maxsize=None)
def _attn_bias(n_heads, dilation, radius):
    tq, tk = 2 * radius, 4 * radius
    slopes = 2.0 ** (-8.0 * (np.arange(n_heads) + 1) / n_heads)
    i = np.arange(tq)[:, None]
    j = np.arange(tk)[None, :]
    cases = []
    for off in (0, radius, 2 * radius):
        rel = np.abs(j - off - i)
        per_head = [np.where(rel <= radius, -slopes[h] * rel * dilation, NEG_BIG) for h in range(n_heads)]
        cases.append(np.stack([np.concatenate(per_head[2 * p:2 * p + 2], axis=0) for p in range(n_heads // 2)]))
    return jnp.asarray(np.stack(cases), dtype=F32)


def _ada_kernel(c_ref, w_ref, b_ref, o_ref):
    o_ref[...] = _dot3(c_ref[...], w_ref[...]) + b_ref[...]


def _ada(c, w_ada, b_ada):
    depth, d, n3 = w_ada.shape
    bsz = c.shape[0]
    tn = d
    return pl.pallas_call(
        _ada_kernel,
        grid=(depth, n3 // tn),
        in_specs=[
            pl.BlockSpec((bsz, d), lambda l, j: (0, 0)),
            pl.BlockSpec((None, d, tn), lambda l, j: (l, 0, j)),
            pl.BlockSpec((None, 1, tn), lambda l, j: (l, 0, j)),
        ],
        out_specs=pl.BlockSpec((None, bsz, tn), lambda l, j: (l, 0, j)),
        out_shape=jax.ShapeDtypeStruct((depth, bsz, n3), F32),
        name="ada_mod",
    )(c, w_ada, b_ada.reshape(depth, 1, n3))


def _inproj_kernel(x_ref, mod_ref, w_ref, b_ref, o_ref, h_ref):
    @pl.when(pl.program_id(2) == 0)
    def _():
        hn = _layer_norm(x_ref[...])
        shift, scale = mod_ref[0:1, :], mod_ref[1:2, :]
        h_ref[...] = (hn * (1.0 + scale) + shift).astype(BF16)

    o_ref[...] = (_mm(h_ref[...], w_ref[...]) + b_ref[...]).astype(o_ref.dtype)


def _inproj(x, mod, w, b):
    bsz, seq, d = x.shape
    n = w.shape[1]
    tm = ROW_TILE
    col = lambda rows: pl.BlockSpec((rows, COL_TILE), lambda bi, i, j: (0, j))
    return pl.pallas_call(
        _inproj_kernel,
        grid=(bsz, seq // tm, n // COL_TILE),
        in_specs=[
            pl.BlockSpec((None, tm, d), lambda bi, i, j: (bi, i, 0)),
            pl.BlockSpec((None, 3, d), lambda bi, i, j: (bi, 0, 0)),
            col(d), col(1),
        ],
        out_specs=pl.BlockSpec((None, tm, COL_TILE), lambda bi, i, j: (bi, i, j)),
        out_shape=jax.ShapeDtypeStruct((bsz, seq, n), BF16),
        scratch_shapes=[pltpu.VMEM((tm, d), BF16)],
        compiler_params=pltpu.CompilerParams(dimension_semantics=("parallel", "parallel", "arbitrary")),
        name="inproj",
    )(x, mod, w, b)


def _attn_kernel(q_ref, k_ref, v_ref, bias_ref, o_ref, lse_ref, *scratch, dilation, seq, radius):
    tq, tk = 2 * radius, 4 * radius
    d = dilation
    length = seq // d
    nblk = length // tq
    first_head = lax.broadcasted_iota(jnp.int32, (tq, LANES), 1) < HEAD_DIM
    pre = max(d // ATTN_STRIDE, 1)
    inner, sub = d // pre, seq // pre
    if d > 1:
        qf, kf, vf, of, lf = scratch[:5]
        for ref, buf in ((q_ref, qf), (k_ref, kf), (v_ref, vf)):
            if pre == 1:
                buf[...] = ref[...].astype(F32)
            else:
                tmp = scratch[5]
                tmp[...] = ref[...].astype(F32)
                for r1 in range(pre):
                    buf[pl.ds(r1 * sub, sub), :] = tmp[pl.ds(r1, sub, stride=pre), :]

    def rows(r, start, n):
        if d == 1:
            return pl.ds(pl.multiple_of(start, radius), n)
        r1, r2 = r & (pre - 1), lax.shift_right_logical(r, pre.bit_length() - 1)
        return pl.ds(r1 * sub + r2 + inner * start, n, stride=inner)

    def block(i, carry):
        r, jb = lax.shift_right_logical(i, nblk.bit_length() - 1), i & (nblk - 1)
        q0 = jb * tq
        ks = jnp.clip(q0 - radius, 0, length - tk)
        case = jnp.where(jb == 0, 0, jnp.where(jb == nblk - 1, 2, 1))
        if d == 1:
            q2, k2, v2 = q_ref[rows(r, q0, tq), :], k_ref[rows(r, ks, tk), :], v_ref[rows(r, ks, tk), :]
        else:
            q2, k2, v2 = (ref[rows(r, st, n), :].astype(BF16) for ref, st, n in ((qf, q0, tq), (kf, ks, tk), (vf, ks, tk)))
        q2 = q2 * (HEAD_DIM ** -0.5)
        zero = jnp.zeros_like(q2)
        qs = jnp.concatenate([jnp.where(first_head, q2, zero), jnp.where(first_head, zero, q2)], axis=0)
        s = lax.dot_general(qs, k2, (((1,), (1,)), ((), ())), preferred_element_type=F32) + bias_ref[case]
        m = jnp.max(s, axis=1, keepdims=True)
        e = jnp.exp(s - m).astype(BF16)
        res = _mm(e, jnp.concatenate([v2, jnp.ones_like(v2)], axis=1))
        den = jnp.where(first_head, res[:tq, LANES:], res[tq:, LANES:])
        o = jnp.where(first_head, res[:tq, :LANES], res[tq:, :LANES]) / den
        lse = jnp.where(first_head, m[:tq], m[tq:]) + jnp.log(den)
        if d == 1:
            o_ref[rows(r, q0, tq), :] = o.astype(o_ref.dtype)
            lse_ref[rows(r, q0, tq), :] = lse
        else:
            of[rows(r, q0, tq), :] = o
            lf[rows(r, q0, tq), :] = lse
        return carry

    lax.fori_loop(0, d * nblk, block, 0, unroll=ATTN_UNROLL)
    if d > 1 and pre == 1:
        o_ref[...] = of[...].astype(o_ref.dtype)
        lse_ref[...] = lf[...]
    elif d > 1:
        tmp = scratch[5]
        for r1 in range(pre):
            tmp[pl.ds(r1, sub, stride=pre), :] = of[pl.ds(r1 * sub, sub), :]
            lse_ref[pl.ds(r1, sub, stride=pre), :] = lf[pl.ds(r1 * sub, sub), :]
        o_ref[...] = tmp[...].astype(o_ref.dtype)


def _attention(p, qkv_cols, dilation, radius, n_heads):
    bsz, seq, _ = p.shape
    da = n_heads * HEAD_DIM
    n_pairs = da // LANES
    bias = _attn_bias(n_heads, dilation, radius)
    kern = functools.partial(_attn_kernel, dilation=dilation, seq=seq, radius=radius)
    col = lambda c0: (lambda bi, pr: (bi, 0, c0 // LANES + pr))
    return pl.pallas_call(
        kern,
        grid=(bsz, n_pairs),
        in_specs=[pl.BlockSpec((None, seq, LANES), col(c0)) for c0 in qkv_cols]
        + [pl.BlockSpec((3, None) + bias.shape[2:], lambda bi, pr: (0, pr, 0, 0))],
        out_specs=[pl.BlockSpec((None, seq, LANES), lambda bi, pr: (bi, 0, pr))] * 2,
        out_shape=[jax.ShapeDtypeStruct((bsz, seq, da), BF16), jax.ShapeDtypeStruct((bsz, seq, da), F32)],
        scratch_shapes=[pltpu.VMEM((seq, LANES), F32)] * (0 if dilation == 1 else 5 if dilation <= ATTN_STRIDE else 6),
        name=f"attn_d{dilation}",
    )(p, p, p, bias)


def _filter_kernel(pos_ref, w1_ref, b1_ref, w2_ref, b2_ref, w3_ref, b3_ref, w4_ref, fr_ref, dec_ref, o_ref):
    fr = fr_ref[...]
    h = jnp.sin(fr * (_dot3(pos_ref[...], w1_ref[...]) + b1_ref[...]))
    h = jnp.sin(fr * (_dot3(h, w2_ref[...]) + b2_ref[...]))
    h = jnp.sin(fr * (_dot3(h, w3_ref[...]) + b3_ref[...]))
    f = _dot3(h, w4_ref[...])
    dec = dec_ref[...]
    c = dec.shape[1]
    row = pl.program_id(0) * pos_ref.shape[0] + lax.broadcasted_iota(jnp.int32, (pos_ref.shape[0], 1), 0)
    for o in range(HYENA_ORDER):
        fwd = f[:, (2 * o) * c:(2 * o + 1) * c] * dec
        bwd = jnp.where(row == 0, 0.0, f[:, (2 * o + 1) * c:(2 * o + 2) * c] * dec)
        o_ref[:, (2 * o) * c:(2 * o + 1) * c] = fwd + bwd
        o_ref[:, (2 * o + 1) * c:(2 * o + 2) * c] = bwd - fwd


def _filters(seq, c, w1, b1, w2, b2, w3, b3, w4, freq):
    pos, decay = _filter_consts(seq, c)
    tm = 512
    w1p = jnp.pad(w1, ((0, PAD_EMB - FILTER_EMB), (0, 0)))
    full = lambda a: pl.BlockSpec(a.shape, lambda i: (0,) * a.ndim)
    args = (w1p, b1.reshape(1, -1), w2, b2.reshape(1, -1), w3, b3.reshape(1, -1), w4, freq.reshape(1, -1))
    return pl.pallas_call(
        _filter_kernel,
        grid=(seq // tm,),
        in_specs=[pl.BlockSpec((tm, PAD_EMB), lambda i: (i, 0))] + [full(a) for a in args]
        + [pl.BlockSpec((tm, c), lambda i: (i, 0))],
        out_specs=pl.BlockSpec((tm, 2 * HYENA_ORDER * c), lambda i: (i, 0)),
        out_shape=jax.ShapeDtypeStruct((seq, 2 * HYENA_ORDER * c), F32),
        name="hyena_filters",
    )(pos, *args, decay)


def _filter_spectrum_kernel(e_ref, d_ref, g_ref, f1r_ref, h_ref, v_buf, s_buf, *, seq):
    t = pl.program_id(2)
    nt = seq // HY_ROWS
    na_half = seq // NB
    na_full = 2 * na_half
    groups = HY_ROWS // NB

    @pl.when(t < nt)
    def _():
        for s, ref in enumerate((e_ref, d_ref)):
            for j in range(groups):
                rows = pl.ds(pl.multiple_of((t * groups + j) * PITCH_T, SUBLANES), NB)
                v_buf[s, rows, :] = ref[j * NB:(j + 1) * NB, :]

    @pl.when(t == nt)
    def _():
        def body(nb, carry):
            rhs = jnp.concatenate([v_buf[s, pl.ds(nb, na_half, stride=PITCH_T), :] for s in range(2)], axis=1)
            y = _mm(f1r_ref[...], rhs.astype(BF16))
            for ri in range(2):
                for s in range(2):
                    s_buf[s, pl.ds(ri * NB + nb, na_full, stride=PITCH_F), :] = (
                        y[ri * na_full:(ri + 1) * na_full, s * LANES:(s + 1) * LANES])
            return carry
        lax.fori_loop(0, NB, body, 0, unroll=HY_UNROLL)

    @pl.when(t > nt)
    def _():
        def body(l, carry):
            rows = pl.ds(pl.multiple_of(((t - nt - 1) * HY_KA + l) * PITCH_F, SUBLANES), 2 * NB)
            rhs = jnp.concatenate([s_buf[s, rows, :] for s in range(2)], axis=1)
            x = _mm(g_ref[l], rhs.astype(BF16))
            h_ref[l, 0:NB, :] = x[:NB, :LANES].astype(h_ref.dtype)
            h_ref[l, NB:2 * NB, :] = (-x[NB:, LANES:]).astype(h_ref.dtype)
            return carry
        lax.fori_loop(0, HY_KA, body, 0, unroll=HY_UNROLL)


def _filter_spectra(filt, seq, c):
    _, f1r, _, g, _ = _dft_consts(seq)
    na_full = 2 * seq // NB
    nt, nk = seq // HY_ROWS, na_full // HY_KA
    slabs = c // LANES
    tile_of = lambda t: jnp.minimum(t, nt - 1)
    freq_of = lambda t: jnp.clip(t - nt - 1, 0, nk - 1)
    return pl.pallas_call(
        functools.partial(_filter_spectrum_kernel, seq=seq),
        grid=(HYENA_ORDER, slabs, nt + 1 + nk),
        in_specs=[pl.BlockSpec((HY_ROWS, LANES), lambda o, s, t: (tile_of(t), (2 * o) * slabs + s)),
                  pl.BlockSpec((HY_ROWS, LANES), lambda o, s, t: (tile_of(t), (2 * o + 1) * slabs + s)),
                  pl.BlockSpec((HY_KA, 2 * NB, 2 * NB), lambda o, s, t: (freq_of(t), 0, 0)),
                  pl.BlockSpec(f1r.shape, lambda o, s, t: (0, 0))],
        out_specs=pl.BlockSpec((None, HY_KA, 2 * NB, LANES), lambda o, s, t: (o, freq_of(t), 0, s)),
        out_shape=jax.ShapeDtypeStruct((HYENA_ORDER, na_full, 2 * NB, c), BF16),
        scratch_shapes=[pltpu.VMEM((2, (na_full // 2) * PITCH_T, LANES), F32),
                        pltpu.VMEM((2, na_full * PITCH_F, LANES), F32)],
        compiler_params=pltpu.CompilerParams(dimension_semantics=("parallel", "parallel", "arbitrary")),
        name="hyena_filter_spectrum",
    )(filt, filt, g, f1r)


def _hyena_schedule(seq):
    nt, nk = seq // HY_ROWS, (2 * seq // NB) // HY_KA
    return {"spec0": nt, "mid": nt + nk, "spec1": 2 * nt + nk, "last": 2 * nt + 2 * nk, "total": 3 * nt + 2 * nk,
            "nt": nt, "nk": nk}


def _hyena_kernel(p_ref, prev_ref, next_ref, gh_ref, cw_ref, cb_ref, fb_ref, hf_ref, g_ref, gi_ref, f1_ref, f4_ref,
                  o_ref, v_buf, y_buf, s_buf, *, seq):
    sch = _hyena_schedule(seq)
    t = pl.program_id(2)
    n_slab = HY_LANES // LANES
    na_half = seq // NB
    na_full, pair_rows = 2 * na_half, 2 * na_half
    groups = HY_ROWS // NB
    lanes = lambda s: slice(s * LANES, (s + 1) * LANES)
    row = lax.broadcasted_iota(jnp.int32, (HY_ROWS, 1), 0)

    def short_conv(tile, b):
        a = p_ref[b].astype(F32)
        top = jnp.where(tile > 0, prev_ref[b, SUBLANES - 1:SUBLANES, :].astype(F32), 0.0)
        bot = jnp.where(tile < sch["nt"] - 1, next_ref[b, 0:1, :].astype(F32), 0.0)
        up = jnp.where(row == 0, top, pltpu.roll(a, 1, 0))
        dn = jnp.where(row == HY_ROWS - 1, bot, pltpu.roll(a, HY_ROWS - 1, 0))
        return cw_ref[0:1, :] * up + cw_ref[1:2, :] * a + cw_ref[2:3, :] * dn + cb_ref[...]

    def time_rows(b, tile, j):
        return pl.ds(pl.multiple_of((b * na_half + tile * groups + j) * PITCH_T, SUBLANES), NB)

    def forward_stage1():
        def body(nb, carry):
            rhs = jnp.concatenate([v_buf[s, pl.ds(nb, pair_rows, stride=PITCH_T), :] for s in range(n_slab)], axis=1)
            y = _mm(f1_ref[...], rhs.astype(BF16))
            for ri in range(2):
                for s in range(n_slab):
                    s_buf[s, pl.ds(ri * NB + nb, na_full, stride=PITCH_F), :] = y[ri * na_full:(ri + 1) * na_full, lanes(s)]
            return carry
        lax.fori_loop(0, NB, body, 0, unroll=HY_UNROLL)

    def spectral(step):
        def body(l, carry):
            rows = pl.ds(pl.multiple_of((step * HY_KA + l) * PITCH_F, SUBLANES), 2 * NB)
            rhs = jnp.concatenate([s_buf[s, rows, :] for s in range(n_slab)], axis=1)
            x = _mm(g_ref[l], rhs.astype(BF16))
            xr, xi = x[:NB], x[NB:]
            hr, hi = hf_ref[l, 0:NB, :].astype(F32), hf_ref[l, NB:2 * NB, :].astype(F32)
            prod = jnp.concatenate([xr * hr - xi * hi, xr * hi + xi * hr], axis=0)
            tt = _mm(gi_ref[l], prod.astype(BF16))
            for s in range(n_slab):
                s_buf[s, rows, :] = tt[:, lanes(s)]
            return carry
        lax.fori_loop(0, HY_KA, body, 0, unroll=HY_UNROLL)

    def inverse_stage():
        def body(nb, carry):
            parts = [jnp.concatenate([s_buf[s, pl.ds(ri * NB + nb, na_full, stride=PITCH_F), :] for s in range(n_slab)],
                                     axis=1) for ri in range(2)]
            y = _mm(f4_ref[...], jnp.concatenate(parts, axis=0).astype(BF16))
            for s in range(n_slab):
                y_buf[s, pl.ds(nb, pair_rows, stride=PITCH_T), :] = y[:, lanes(s)]
            return carry
        lax.fori_loop(0, NB, body, 0, unroll=HY_UNROLL)

    @pl.when(t < sch["spec0"])
    def _():
        for b in range(2):
            v = short_conv(t, b)
            for j in range(groups):
                for s in range(n_slab):
                    v_buf[s, time_rows(b, t, j), :] = v[j * NB:(j + 1) * NB, lanes(s)]

    @pl.when((t >= sch["mid"]) & (t < sch["spec1"]))
    def _():
        tile = t - sch["mid"]
        for b in range(2):
            x1 = short_conv(tile, b)
            for j in range(groups):
                for s in range(n_slab):
                    rows = time_rows(b, tile, j)
                    z = x1[j * NB:(j + 1) * NB, lanes(s)] * (y_buf[s, rows, :] + fb_ref[:, lanes(s)] * v_buf[s, rows, :])
                    v_buf[s, rows, :] = z

    @pl.when((t == sch["spec0"] - 1) | (t == sch["spec1"] - 1))
    def _():
        forward_stage1()

    @pl.when((t >= sch["spec0"]) & (t < sch["mid"]))
    def _():
        spectral(t - sch["spec0"])

    @pl.when((t >= sch["spec1"]) & (t < sch["last"]))
    def _():
        spectral(t - sch["spec1"])

    @pl.when((t == sch["mid"] - 1) | (t == sch["last"] - 1))
    def _():
        inverse_stage()

    @pl.when(t >= sch["last"])
    def _():
        tile = t - sch["last"]
        for b in range(2):
            x2 = short_conv(tile, b) * _silu(gh_ref[b].astype(F32))
            for j in range(groups):
                for s in range(n_slab):
                    rows = time_rows(b, tile, j)
                    z = x2[j * NB:(j + 1) * NB, lanes(s)] * (y_buf[s, rows, :] + fb_ref[:, lanes(s)] * v_buf[s, rows, :])
                    o_ref[b, j * NB:(j + 1) * NB, lanes(s)] = z.astype(o_ref.dtype)


def _hyena(p, col_hy, col_gate, conv_w, conv_b, filt_bias, hf, c):
    bsz, seq, ncols = p.shape
    f1, _, f4, g, gi = _dft_consts(seq)
    sch = _hyena_schedule(seq)
    nt, nk = sch["nt"], sch["nk"]
    n_pairs, na_full = bsz // 2, 2 * seq // NB
    groups_per_c = c // HY_LANES
    halo_per_tile = HY_ROWS // SUBLANES
    pv = p.reshape(n_pairs, 2, seq, ncols)

    def tile_of(t):
        return jnp.where(t < sch["mid"], jnp.minimum(t, nt - 1),
                         jnp.where(t < sch["last"], jnp.clip(t - sch["mid"], 0, nt - 1), jnp.clip(t - sch["last"], 0, nt - 1)))

    def col_of(t, lg):
        return jnp.where(t < sch["mid"], 0, jnp.where(t < sch["last"], 1, 2)) * groups_per_c + lg

    p_col = lambda t, lg: col_hy // HY_LANES + col_of(t, lg)

    def freq_of(t):
        return jnp.where(t < sch["spec1"], jnp.clip(t - sch["spec0"], 0, nk - 1), jnp.clip(t - sch["spec1"], 0, nk - 1))

    order_of = lambda t: jnp.where(t < sch["spec1"], 0, 1)
    last_tile = lambda t: jnp.clip(t - sch["last"], 0, nt - 1)
    const = lambda a: pl.BlockSpec(a.shape, lambda bp, lg, t: (0,) * a.ndim)
    width = 3 * c
    out = pl.pallas_call(
        functools.partial(_hyena_kernel, seq=seq),
        grid=(n_pairs, groups_per_c, sch["total"]),
        in_specs=[
            pl.BlockSpec((None, 2, HY_ROWS, HY_LANES), lambda bp, lg, t: (bp, 0, tile_of(t), p_col(t, lg))),
            pl.BlockSpec((None, 2, SUBLANES, HY_LANES),
                         lambda bp, lg, t: (bp, 0, jnp.maximum(tile_of(t) * halo_per_tile - 1, 0), p_col(t, lg))),
            pl.BlockSpec((None, 2, SUBLANES, HY_LANES),
                         lambda bp, lg, t: (bp, 0, jnp.minimum((tile_of(t) + 1) * halo_per_tile, seq // SUBLANES - 1),
                                            p_col(t, lg))),
            pl.BlockSpec((None, 2, HY_ROWS, HY_LANES), lambda bp, lg, t: (bp, 0, last_tile(t), col_gate // HY_LANES + lg)),
            pl.BlockSpec((3, HY_LANES), lambda bp, lg, t: (0, col_of(t, lg))),
            pl.BlockSpec((1, HY_LANES), lambda bp, lg, t: (0, col_of(t, lg))),
            pl.BlockSpec((None, 1, HY_LANES), lambda bp, lg, t: (order_of(t), 0, lg)),
            pl.BlockSpec((None, HY_KA, 2 * NB, HY_LANES), lambda bp, lg, t: (order_of(t), freq_of(t), 0, lg)),
            pl.BlockSpec((HY_KA, 2 * NB, 2 * NB), lambda bp, lg, t: (freq_of(t), 0, 0)),
            pl.BlockSpec((HY_KA, 2 * NB, 2 * NB), lambda bp, lg, t: (freq_of(t), 0, 0)),
            const(f1), const(f4),
        ],
        out_specs=pl.BlockSpec((None, 2, HY_ROWS, HY_LANES), lambda bp, lg, t: (bp, 0, last_tile(t), lg)),
        out_shape=jax.ShapeDtypeStruct((n_pairs, 2, seq, c), BF16),
        scratch_shapes=[
            pltpu.VMEM((HY_LANES // LANES, na_full * PITCH_T, LANES), F32),
            pltpu.VMEM((HY_LANES // LANES, na_full * PITCH_T, LANES), F32),
            pltpu.VMEM((HY_LANES // LANES, na_full * PITCH_F, LANES), F32),
        ],
        compiler_params=pltpu.CompilerParams(dimension_semantics=("parallel", "parallel", "arbitrary")),
        name="hyena_long_conv",
    )(pv, pv, pv, pv, conv_w, conv_b.reshape(1, width), filt_bias.reshape(HYENA_ORDER, 1, c), hf, g, gi, f1, f4)
    return out.reshape(bsz, seq, c)


def _out_kernel(o0_ref, o1_ref, o2_ref, l0_ref, l1_ref, l2_ref, ga_ref, yh_ref, ra_ref, rh_ref, x_ref, mod_ref,
                wpa_ref, wph_ref, wo_ref, bo_ref, lg_ref, lb_ref, out_ref, *, alpha):
    lses = [l0_ref[...], l1_ref[...], l2_ref[...]]
    m = jnp.maximum(jnp.maximum(lses[0], lses[1]), lses[2])
    es = [jnp.exp(l - m) for l in lses]
    o = es[0] * o0_ref[...].astype(F32) + es[1] * o1_ref[...].astype(F32) + es[2] * o2_ref[...].astype(F32)
    o = o / (es[0] + es[1] + es[2])
    ya = (o * _silu(ga_ref[...].astype(F32))).astype(BF16)
    merged = (jax.nn.sigmoid(ra_ref[...].astype(F32)) * _mm(ya, wpa_ref[...])
              + jax.nn.sigmoid(rh_ref[...].astype(F32)) * _mm(yh_ref[...], wph_ref[...]))
    out = _mm(merged.astype(BF16), wo_ref[...]) + bo_ref[...]
    res = alpha * x_ref[...] + mod_ref[2:3, :] * out
    out_ref[...] = _layer_norm(res) * lg_ref[...] + lb_ref[...]


def _out_proj(os_, lses, p, cols, yh, x, mod, wpa, wph, wo, bo, lg, lb, alpha):
    bsz, seq, d = x.shape
    da = wpa.shape[0]
    tm = OUT_TILE
    rows = lambda width, cb=0: pl.BlockSpec((None, tm, width), lambda bi, i: (bi, i, cb))
    const = lambda a: pl.BlockSpec(a.shape, lambda bi, i: (0,) * a.ndim)
    ga_col, ra_col, rh_col = cols
    return pl.pallas_call(
        functools.partial(_out_kernel, alpha=alpha),
        grid=(bsz, seq // tm),
        in_specs=[rows(da)] * 6
        + [rows(da, ga_col // da), rows(da), rows(d, ra_col // d), rows(d, rh_col // d), rows(d),
           pl.BlockSpec((None, 3, d), lambda bi, i: (bi, 0, 0)),
           const(wpa), const(wph), const(wo)]
        + [pl.BlockSpec((1, d), lambda bi, i: (0, 0))] * 3,
        out_specs=rows(d),
        out_shape=jax.ShapeDtypeStruct((bsz, seq, d), F32),
        name="merge_out_proj",
    )(*os_, *lses, p, yh, p, p, x, mod, wpa, wph, wo, bo.reshape(1, d), lg.reshape(1, d), lb.reshape(1, d))


def _layer(x, mod, w_in, b_in, conv_w, conv_b, fw1, fb1, fw2, fb2, fw3, fb3, fw4, ffreq, fbias,
           w_proj_attn, w_proj_hyena, w_out, b_out, ln_g, ln_b, alpha):
    bsz, seq, d = x.shape
    da = w_proj_attn.shape[0]
    c = w_proj_hyena.shape[0]
    n_heads = da // HEAD_DIM

    col_ga = 3 * N_GROUPS * da
    col_hy = col_ga + da
    col_gh = col_hy + 3 * c
    col_ra = col_gh + c
    col_rh = col_ra + d

    p = _inproj(x, mod, w_in.astype(BF16), b_in.reshape(1, -1))

    outs, lses = [], []
    for g, (window, dilation) in enumerate(DILATED_GROUPS):
        qkv_cols = [(i * N_GROUPS + g) * da for i in range(3)]
        o_g, lse_g = _attention(p, qkv_cols, dilation, window // (2 * dilation), n_heads)
        outs.append(o_g)
        lses.append(lse_g)

    filt = _filters(seq, c, fw1, fb1, fw2, fb2, fw3, fb3, fw4, ffreq)
    hf = _filter_spectra(filt, seq, c)
    yh = _hyena(p, col_hy, col_gh, conv_w, conv_b, fbias, hf, c)

    return _out_proj(outs, lses, p, (col_ga, col_ra, col_rh), yh, x, mod,
                     w_proj_attn.astype(BF16), w_proj_hyena.astype(BF16), w_out.astype(BF16),
                     b_out, ln_g, ln_b, alpha)


def kernel(x, c, w_ada, b_ada, w_in, b_in, conv_w, conv_b, filt_w1, filt_b1, filt_w2, filt_b2, filt_w3, filt_b3,
           filt_w4, filt_freq, filt_bias, w_proj_attn, w_proj_hyena, w_out, b_out, ln_g, ln_b):
    depth = w_ada.shape[0]
    alpha = (2 * depth) ** 0.25
    d = x.shape[-1]
    mods = _ada(c, w_ada, b_ada).reshape(depth, c.shape[0], 3, d)
    for l in range(depth):
        x = _layer(x, mods[l], w_in[l], b_in[l], conv_w[l], conv_b[l], filt_w1[l], filt_b1[l], filt_w2[l],
                   filt_b2[l], filt_w3[l], filt_b3[l], filt_w4[l], filt_freq[l], filt_bias[l],
                   w_proj_attn[l], w_proj_hyena[l], w_out[l], b_out[l], ln_g[l], ln_b[l], alpha)
    return x
```

```python
import functools
import math

import numpy as np
import jax
import jax.numpy as jnp
from jax import lax
from jax.experimental import pallas as pl
from jax.experimental.pallas import tpu as pltpu

F32 = jnp.float32
BF16 = jnp.bfloat16

HEAD_DIM = 64
DILATED_GROUPS = ((128, 1), (512, 4), (2048, 16))
N_GROUPS = len(DILATED_GROUPS)
HYENA_ORDER = 2
FILTER_EMB = 33
FILTER_WIDTH = 64
DECAY_TARGET = 1e-2
FAST_DECAY_PCT = 0.3
SLOW_DECAY_PCT = 1.5
LN_EPS = 1e-5

LANES = 128
NB = 64
PAD_EMB = 128
NEG_BIG = -1e30

ROW_TILE = 2048
COL_TILE = 1536
OUT_TILE = 512
ATTN_UNROLL = 16
ATTN_STRIDE = 4
HY_LANES = 256
HY_ROWS = 1024
HY_KA = 32
FS_ROWS = 4096
FS_KA = 64
HY_UNROLL = 32
SUBLANES = 8
PITCH_T = NB + SUBLANES
PITCH_F = 2 * NB + SUBLANES


def _mm(a, b):
    return jnp.dot(a, b, preferred_element_type=F32)


def _split(a):
    hi = a.astype(BF16)
    lo = (a - hi.astype(F32)).astype(BF16)
    return hi, lo


def _dot3(a, b):
    ah, al = _split(a)
    bh, bl = _split(b)
    return _mm(ah, bh) + _mm(ah, bl) + _mm(al, bh)


def _layer_norm(x):
    mu = jnp.mean(x, axis=-1, keepdims=True)
    xc = x - mu
    var = jnp.mean(xc * xc, axis=-1, keepdims=True)
    return xc * lax.rsqrt(var + LN_EPS)


def _sigmoid(a):
    return 0.5 * jnp.tanh(0.5 * a) + 0.5


def _silu(a):
    return a * _sigmoid(a)


@functools.lru_cache(maxsize=None)
def _dft_consts(seq):
    n = 2 * seq
    na_full = n // NB
    na_half = na_full // 2
    ka = np.arange(na_full)[:, None]
    na = np.arange(na_half)[None, :]
    ang = 2.0 * np.pi * ((ka * na) % na_full) / na_full
    c, s = np.cos(ang), np.sin(ang)
    f1 = np.block([[c, s], [-s, c]])
    f1r = np.concatenate([c, -s], axis=0)
    f4 = np.block([[c.T, -s.T], [s.T, c.T]]) / n
    kav = np.arange(na_full)[:, None, None]
    kb = np.arange(NB)[None, :, None]
    nb = np.arange(NB)[None, None, :]
    th = 2.0 * np.pi * (((nb * kav) % n) / n + ((nb * kb) % NB) / NB)
    cg, sg = np.cos(th), np.sin(th)
    g = np.concatenate([np.concatenate([cg, sg], axis=2), np.concatenate([-sg, cg], axis=2)], axis=1)
    cgt, sgt = cg.transpose(0, 2, 1), sg.transpose(0, 2, 1)
    gi = np.concatenate([np.concatenate([cgt, -sgt], axis=2), np.concatenate([sgt, cgt], axis=2)], axis=1)
    as_bf16 = lambda a: jnp.asarray(a, dtype=F32).astype(BF16)
    return as_bf16(f1), as_bf16(f1r), as_bf16(f4), as_bf16(g), as_bf16(gi)


@functools.lru_cache(maxsize=None)
def _filter_consts(seq, channels):
    t = np.linspace(0.0, 1.0, seq)[:, None]
    bands = (FILTER_EMB - 1) // 2
    w = 2.0 * np.pi * np.arange(seq)[:, None] / seq
    f = np.linspace(1e-4, bands - 1, bands)[None, :]
    pos = np.concatenate([t, np.cos(f * w), -np.sin(f * w)], axis=-1)
    pos = np.pad(pos, ((0, 0), (0, PAD_EMB - FILTER_EMB)))
    deltas = np.linspace(math.log(DECAY_TARGET) / SLOW_DECAY_PCT, math.log(DECAY_TARGET) / FAST_DECAY_PCT, channels)
    decay = np.exp(-t * np.abs(deltas)[None, :])
    return jnp.asarray(pos, dtype=F32), jnp.asarray(decay, dtype=F32)


@functools.lru_cache(maxsize=None)
def _attn_bias(n_heads, dilation, radius):
    tq, tk = 2 * radius, 4 * radius
    slopes = 2.0 ** (-8.0 * (np.arange(n_heads) + 1) / n_heads)
    i = np.arange(tq)[:, None]
    j = np.arange(tk)[None, :]
    cases = []
    for off in (0, radius, 2 * radius):
        rel = np.abs(j - off - i)
        per_head = [np.where(rel <= radius, -slopes[h] * rel * dilation, NEG_BIG) for h in range(n_heads)]
        cases.append(np.stack([np.concatenate(per_head[2 * p:2 * p + 2], axis=0) for p in range(n_heads // 2)]))
    return jnp.asarray(np.stack(cases), dtype=F32)


def _ada_kernel(c_ref, w_ref, b_ref, o_ref):
    o_ref[...] = _dot3(c_ref[...], w_ref[...]) + b_ref[...]


def _ada(c, w_ada, b_ada):
    depth, d, n3 = w_ada.shape
    bsz = c.shape[0]
    tn = d
    return pl.pallas_call(
        _ada_kernel,
        grid=(depth, n3 // tn),
        in_specs=[
            pl.BlockSpec((bsz, d), lambda l, j: (0, 0)),
            pl.BlockSpec((None, d, tn), lambda l, j: (l, 0, j)),
            pl.BlockSpec((None, 1, tn), lambda l, j: (l, 0, j)),
        ],
        out_specs=pl.BlockSpec((None, bsz, tn), lambda l, j: (l, 0, j)),
        out_shape=jax.ShapeDtypeStruct((depth, bsz, n3), F32),
        name="ada_mod",
    )(c, w_ada, b_ada.reshape(depth, 1, n3))


def _inproj_kernel(x_ref, mod_ref, w_ref, b_ref, o_ref, h_ref):
    @pl.when(pl.program_id(2) == 0)
    def _():
        hn = _layer_norm(x_ref[...])
        shift, scale = mod_ref[0:1, :], mod_ref[1:2, :]
        h_ref[...] = (hn * (1.0 + scale) + shift).astype(BF16)

    o_ref[...] = (_mm(h_ref[...], w_ref[...]) + b_ref[...]).astype(o_ref.dtype)


def _inproj(x, mod, w, b):
    bsz, seq, d = x.shape
    n = w.shape[1]
    tm = ROW_TILE
    col = lambda rows: pl.BlockSpec((rows, COL_TILE), lambda bi, i, j: (0, j))
    return pl.pallas_call(
        _inproj_kernel,
        grid=(bsz, seq // tm, n // COL_TILE),
        in_specs=[
            pl.BlockSpec((None, tm, d), lambda bi, i, j: (bi, i, 0)),
            pl.BlockSpec((None, 3, d), lambda bi, i, j: (bi, 0, 0)),
            col(d), col(1),
        ],
        out_specs=pl.BlockSpec((None, tm, COL_TILE), lambda bi, i, j: (bi, i, j)),
        out_shape=jax.ShapeDtypeStruct((bsz, seq, n), BF16),
        scratch_shapes=[pltpu.VMEM((tm, d), BF16)],
        compiler_params=pltpu.CompilerParams(dimension_semantics=("parallel", "parallel", "arbitrary")),
        name="inproj",
    )(x, mod, w, b)


def _attn_kernel(q_ref, k_ref, v_ref, bias_ref, o_ref, lse_ref, *scratch, dilation, seq, radius):
    tq, tk = 2 * radius, 4 * radius
    d = dilation
    length = seq // d
    nblk = length // tq
    first_head = lax.broadcasted_iota(jnp.int32, (tq, LANES), 1) < HEAD_DIM
    pre = max(d // ATTN_STRIDE, 1)
    inner, sub = d // pre, seq // pre
    if d > 1:
        qf, kf, vf, of, lf = scratch[:5]
        for ref, buf in ((q_ref, qf), (k_ref, kf), (v_ref, vf)):
            if pre == 1:
                buf[...] = ref[...].astype(F32)
            else:
                tmp = scratch[5]
                tmp[...] = ref[...].astype(F32)
                for r1 in range(pre):
                    buf[pl.ds(r1 * sub, sub), :] = tmp[pl.ds(r1, sub, stride=pre), :]

    def rows(r, start, n):
        if d == 1:
            return pl.ds(pl.multiple_of(start, radius), n)
        r1, r2 = r & (pre - 1), lax.shift_right_logical(r, pre.bit_length() - 1)
        return pl.ds(r1 * sub + r2 + inner * start, n, stride=inner)

    def block(i, carry):
        r, jb = lax.shift_right_logical(i, nblk.bit_length() - 1), i & (nblk - 1)
        q0 = jb * tq
        ks = jnp.clip(q0 - radius, 0, length - tk)
        case = jnp.where(jb == 0, 0, jnp.where(jb == nblk - 1, 2, 1))
        if d == 1:
            q2, k2, v2 = q_ref[rows(r, q0, tq), :], k_ref[rows(r, ks, tk), :], v_ref[rows(r, ks, tk), :]
        else:
            q2, k2, v2 = (ref[rows(r, st, n), :].astype(BF16) for ref, st, n in ((qf, q0, tq), (kf, ks, tk), (vf, ks, tk)))
        q2 = q2 * (HEAD_DIM ** -0.5)
        zero = jnp.zeros_like(q2)
        qs = jnp.concatenate([jnp.where(first_head, q2, zero), jnp.where(first_head, zero, q2)], axis=0)
        s = lax.dot_general(qs, k2, (((1,), (1,)), ((), ())), preferred_element_type=F32) + bias_ref[case]
        m = jnp.max(s, axis=1, keepdims=True)
        e = jnp.exp(s - m).astype(BF16)
        res = _mm(e, jnp.concatenate([v2, jnp.ones_like(v2)], axis=1))
        den = jnp.where(first_head, res[:tq, LANES:], res[tq:, LANES:])
        o = jnp.where(first_head, res[:tq, :LANES], res[tq:, :LANES]) / den
        lse = jnp.where(first_head, m[:tq], m[tq:]) + jnp.log(den)
        if d == 1:
            o_ref[rows(r, q0, tq), :] = o.astype(o_ref.dtype)
            lse_ref[rows(r, q0, tq), :] = lse
        else:
            of[rows(r, q0, tq), :] = o
            lf[rows(r, q0, tq), :] = lse
        return carry

    lax.fori_loop(0, d * nblk, block, 0, unroll=ATTN_UNROLL)
    if d > 1 and pre == 1:
        o_ref[...] = of[...].astype(o_ref.dtype)
        lse_ref[...] = lf[...]
    elif d > 1:
        tmp = scratch[5]
        for r1 in range(pre):
            tmp[pl.ds(r1, sub, stride=pre), :] = of[pl.ds(r1 * sub, sub), :]
            lse_ref[pl.ds(r1, sub, stride=pre), :] = lf[pl.ds(r1 * sub, sub), :]
        o_ref[...] = tmp[...].astype(o_ref.dtype)


def _attention(p, qkv_cols, dilation, radius, n_heads):
    bsz, seq, _ = p.shape
    da = n_heads * HEAD_DIM
    n_pairs = da // LANES
    bias = _attn_bias(n_heads, dilation, radius)
    kern = functools.partial(_attn_kernel, dilation=dilation, seq=seq, radius=radius)
    col = lambda c0: (lambda bi, pr: (bi, 0, c0 // LANES + pr))
    return pl.pallas_call(
        kern,
        grid=(bsz, n_pairs),
        in_specs=[pl.BlockSpec((None, seq, LANES), col(c0)) for c0 in qkv_cols]
        + [pl.BlockSpec((3, None) + bias.shape[2:], lambda bi, pr: (0, pr, 0, 0))],
        out_specs=[pl.BlockSpec((None, seq, LANES), lambda bi, pr: (bi, 0, pr))] * 2,
        out_shape=[jax.ShapeDtypeStruct((bsz, seq, da), BF16), jax.ShapeDtypeStruct((bsz, seq, da), F32)],
        scratch_shapes=[pltpu.VMEM((seq, LANES), F32)] * (0 if dilation == 1 else 5 if dilation <= ATTN_STRIDE else 6),
        name=f"attn_d{dilation}",
    )(p, p, p, bias)


def _filter_kernel(pos_ref, w1_ref, b1_ref, w2_ref, b2_ref, w3_ref, b3_ref, w4_ref, fr_ref, dec_ref, o_ref):
    fr = fr_ref[...]
    h = jnp.sin(fr * (_dot3(pos_ref[...], w1_ref[...]) + b1_ref[...]))
    h = jnp.sin(fr * (_dot3(h, w2_ref[...]) + b2_ref[...]))
    h = jnp.sin(fr * (_dot3(h, w3_ref[...]) + b3_ref[...]))
    f = _dot3(h, w4_ref[...])
    dec = dec_ref[...]
    c = dec.shape[1]
    row = pl.program_id(0) * pos_ref.shape[0] + lax.broadcasted_iota(jnp.int32, (pos_ref.shape[0], 1), 0)
    for o in range(HYENA_ORDER):
        fwd = f[:, (2 * o) * c:(2 * o + 1) * c] * dec
        bwd = jnp.where(row == 0, 0.0, f[:, (2 * o + 1) * c:(2 * o + 2) * c] * dec)
        o_ref[:, (2 * o) * c:(2 * o + 1) * c] = fwd + bwd
        o_ref[:, (2 * o + 1) * c:(2 * o + 2) * c] = bwd - fwd


def _filters(seq, c, w1, b1, w2, b2, w3, b3, w4, freq):
    pos, decay = _filter_consts(seq, c)
    tm = 512
    w1p = jnp.pad(w1, ((0, PAD_EMB - FILTER_EMB), (0, 0)))
    full = lambda a: pl.BlockSpec(a.shape, lambda i: (0,) * a.ndim)
    args = (w1p, b1.reshape(1, -1), w2, b2.reshape(1, -1), w3, b3.reshape(1, -1), w4, freq.reshape(1, -1))
    return pl.pallas_call(
        _filter_kernel,
        grid=(seq // tm,),
        in_specs=[pl.BlockSpec((tm, PAD_EMB), lambda i: (i, 0))] + [full(a) for a in args]
        + [pl.BlockSpec((tm, c), lambda i: (i, 0))],
        out_specs=pl.BlockSpec((tm, 2 * HYENA_ORDER * c), lambda i: (i, 0)),
        out_shape=jax.ShapeDtypeStruct((seq, 2 * HYENA_ORDER * c), F32),
        name="hyena_filters",
    )(pos, *args, decay)


def _filter_spectrum_kernel(e_ref, d_ref, g_ref, f1r_ref, h_ref, v_buf, s_buf, *, seq):
    t = pl.program_id(2)
    tile_rows, n_ka = e_ref.shape[0], g_ref.shape[0]
    nt = seq // tile_rows
    na_half = seq // NB
    na_full = 2 * na_half
    groups = tile_rows // NB

    @pl.when(t < nt)
    def _():
        for s, ref in enumerate((e_ref, d_ref)):
            for j in range(groups):
                rows = pl.ds(pl.multiple_of((t * groups + j) * PITCH_T, SUBLANES), NB)
                v_buf[s, rows, :] = ref[j * NB:(j + 1) * NB, :]

    @pl.when(t == nt - 1)
    def _():
        def body(nb, carry):
            rhs = jnp.concatenate([v_buf[s, pl.ds(nb, na_half, stride=PITCH_T), :] for s in range(2)], axis=1)
            y = _mm(f1r_ref[...], rhs.astype(BF16))
            for ri in range(2):
                for s in range(2):
                    s_buf[s, pl.ds(ri * NB + nb, na_full, stride=PITCH_F), :] = (
                        y[ri * na_full:(ri + 1) * na_full, s * LANES:(s + 1) * LANES])
            return carry
        lax.fori_loop(0, NB, body, 0, unroll=HY_UNROLL)

    @pl.when(t >= nt)
    def _():
        def body(l, carry):
            rows = pl.ds(pl.multiple_of(((t - nt) * n_ka + l) * PITCH_F, SUBLANES), 2 * NB)
            rhs = jnp.concatenate([s_buf[s, rows, :] for s in range(2)], axis=1)
            x = _mm(g_ref[l], rhs.astype(BF16))
            h_ref[l, 0:NB, :] = x[:NB, :LANES].astype(h_ref.dtype)
            h_ref[l, NB:2 * NB, :] = (-x[NB:, LANES:]).astype(h_ref.dtype)
            return carry
        lax.fori_loop(0, n_ka, body, 0, unroll=HY_UNROLL)


def _filter_spectra(filt, seq, c):
    _, f1r, _, g, _ = _dft_consts(seq)
    na_full = 2 * seq // NB
    nt, nk = seq // FS_ROWS, na_full // FS_KA
    slabs = c // LANES
    tile_of = lambda t: jnp.minimum(t, nt - 1)
    freq_of = lambda t: jnp.clip(t - nt, 0, nk - 1)
    return pl.pallas_call(
        functools.partial(_filter_spectrum_kernel, seq=seq),
        grid=(HYENA_ORDER, slabs, nt + nk),
        in_specs=[pl.BlockSpec((FS_ROWS, LANES), lambda o, s, t: (tile_of(t), (2 * o) * slabs + s)),
                  pl.BlockSpec((FS_ROWS, LANES), lambda o, s, t: (tile_of(t), (2 * o + 1) * slabs + s)),
                  pl.BlockSpec((FS_KA, 2 * NB, 2 * NB), lambda o, s, t: (freq_of(t), 0, 0)),
                  pl.BlockSpec(f1r.shape, lambda o, s, t: (0, 0))],
        out_specs=pl.BlockSpec((None, FS_KA, 2 * NB, LANES), lambda o, s, t: (o, freq_of(t), 0, s)),
        out_shape=jax.ShapeDtypeStruct((HYENA_ORDER, na_full, 2 * NB, c), BF16),
        scratch_shapes=[pltpu.VMEM((2, (na_full // 2) * PITCH_T, LANES), F32),
                        pltpu.VMEM((2, na_full * PITCH_F, LANES), F32)],
        compiler_params=pltpu.CompilerParams(dimension_semantics=("parallel", "parallel", "arbitrary")),
        name="hyena_filter_spectrum",
    )(filt, filt, g, f1r)


def _hyena_schedule(seq):
    nt, nk = seq // HY_ROWS, (2 * seq // NB) // HY_KA
    return {"spec0": nt, "mid": nt + nk, "spec1": 2 * nt + nk, "last": 2 * nt + 2 * nk, "total": 3 * nt + 2 * nk,
            "nt": nt, "nk": nk}


def _hyena_kernel(p_ref, prev_ref, next_ref, gh_ref, cw_ref, cb_ref, fb_ref, hf_ref, g_ref, gi_ref, f1_ref, f4_ref,
                  o_ref, v_buf, y_buf, s_buf, *, seq):
    sch = _hyena_schedule(seq)
    t = pl.program_id(2)
    n_slab = HY_LANES // LANES
    na_half = seq // NB
    na_full, pair_rows = 2 * na_half, 2 * na_half
    groups = HY_ROWS // NB
    lanes = lambda s: slice(s * LANES, (s + 1) * LANES)
    row = lax.broadcasted_iota(jnp.int32, (HY_ROWS, 1), 0)

    def short_conv(tile, b):
        a = p_ref[b].astype(F32)
        top = jnp.where(tile > 0, prev_ref[b, SUBLANES - 1:SUBLANES, :].astype(F32), 0.0)
        bot = jnp.where(tile < sch["nt"] - 1, next_ref[b, 0:1, :].astype(F32), 0.0)
        up = jnp.where(row == 0, top, pltpu.roll(a, 1, 0))
        dn = jnp.where(row == HY_ROWS - 1, bot, pltpu.roll(a, HY_ROWS - 1, 0))
        return cw_ref[0:1, :] * up + cw_ref[1:2, :] * a + cw_ref[2:3, :] * dn + cb_ref[...]

    def time_rows(b, tile, j):
        return pl.ds(pl.multiple_of((b * na_half + tile * groups + j) * PITCH_T, SUBLANES), NB)

    def forward_stage1():
        def body(nb, carry):
            rhs = jnp.concatenate([v_buf[s, pl.ds(nb, pair_rows, stride=PITCH_T), :] for s in range(n_slab)], axis=1)
            y = _mm(f1_ref[...], rhs.astype(BF16))
            for ri in range(2):
                for s in range(n_slab):
                    s_buf[s, pl.ds(ri * NB + nb, na_full, stride=PITCH_F), :] = y[ri * na_full:(ri + 1) * na_full, lanes(s)]
            return carry
        lax.fori_loop(0, NB, body, 0, unroll=HY_UNROLL)

    def spectral(step):
        def body(l, carry):
            rows = pl.ds(pl.multiple_of((step * HY_KA + l) * PITCH_F, SUBLANES), 2 * NB)
            rhs = jnp.concatenate([s_buf[s, rows, :] for s in range(n_slab)], axis=1)
            x = _mm(g_ref[l], rhs.astype(BF16))
            xr, xi = x[:NB], x[NB:]
            hr, hi = hf_ref[l, 0:NB, :].astype(F32), hf_ref[l, NB:2 * NB, :].astype(F32)
            prod = jnp.concatenate([xr * hr - xi * hi, xr * hi + xi * hr], axis=0)
            tt = _mm(gi_ref[l], prod.astype(BF16))
            for s in range(n_slab):
                s_buf[s, rows, :] = tt[:, lanes(s)]
            return carry
        lax.fori_loop(0, HY_KA, body, 0, unroll=HY_UNROLL)

    def inverse_stage():
        def body(nb, carry):
            parts = [jnp.concatenate([s_buf[s, pl.ds(ri * NB + nb, na_full, stride=PITCH_F), :] for s in range(n_slab)],
                                     axis=1) for ri in range(2)]
            y = _mm(f4_ref[...], jnp.concatenate(parts, axis=0).astype(BF16))
            for s in range(n_slab):
                y_buf[s, pl.ds(nb, pair_rows, stride=PITCH_T), :] = y[:, lanes(s)]
            return carry
        lax.fori_loop(0, NB, body, 0, unroll=HY_UNROLL)

    @pl.when(t < sch["spec0"])
    def _():
        for b in range(2):
            v = short_conv(t, b)
            for j in range(groups):
                for s in range(n_slab):
                    v_buf[s, time_rows(b, t, j), :] = v[j * NB:(j + 1) * NB, lanes(s)]

    @pl.when((t >= sch["mid"]) & (t < sch["spec1"]))
    def _():
        tile = t - sch["mid"]
        for b in range(2):
            x1 = short_conv(tile, b)
            for j in range(groups):
                for s in range(n_slab):
                    rows = time_rows(b, tile, j)
                    z = x1[j * NB:(j + 1) * NB, lanes(s)] * (y_buf[s, rows, :] + fb_ref[:, lanes(s)] * v_buf[s, rows, :])
                    v_buf[s, rows, :] = z

    @pl.when((t == sch["spec0"] - 1) | (t == sch["spec1"] - 1))
    def _():
        forward_stage1()

    @pl.when((t >= sch["spec0"]) & (t < sch["mid"]))
    def _():
        spectral(t - sch["spec0"])

    @pl.when((t >= sch["spec1"]) & (t < sch["last"]))
    def _():
        spectral(t - sch["spec1"])

    @pl.when((t == sch["mid"] - 1) | (t == sch["last"] - 1))
    def _():
        inverse_stage()

    @pl.when(t >= sch["last"])
    def _():
        tile = t - sch["last"]
        for b in range(2):
            x2 = short_conv(tile, b) * _silu(gh_ref[b].astype(F32))
            for j in range(groups):
                for s in range(n_slab):
                    rows = time_rows(b, tile, j)
                    z = x2[j * NB:(j + 1) * NB, lanes(s)] * (y_buf[s, rows, :] + fb_ref[:, lanes(s)] * v_buf[s, rows, :])
                    o_ref[b, j * NB:(j + 1) * NB, lanes(s)] = z.astype(o_ref.dtype)


def _hyena(p, col_hy, col_gate, conv_w, conv_b, filt_bias, hf, c):
    bsz, seq, ncols = p.shape
    f1, _, f4, g, gi = _dft_consts(seq)
    sch = _hyena_schedule(seq)
    nt, nk = sch["nt"], sch["nk"]
    n_pairs, na_full = bsz // 2, 2 * seq // NB
    groups_per_c = c // HY_LANES
    halo_per_tile = HY_ROWS // SUBLANES
    pv = p.reshape(n_pairs, 2, seq, ncols)

    def tile_of(t):
        return jnp.where(t < sch["mid"], jnp.minimum(t, nt - 1),
                         jnp.where(t < sch["last"], jnp.clip(t - sch["mid"], 0, nt - 1), jnp.clip(t - sch["last"], 0, nt - 1)))

    def col_of(t, lg):
        return jnp.where(t < sch["mid"], 0, jnp.where(t < sch["last"], 1, 2)) * groups_per_c + lg

    p_col = lambda t, lg: col_hy // HY_LANES + col_of(t, lg)

    def freq_of(t):
        return jnp.where(t < sch["spec1"], jnp.clip(t - sch["spec0"], 0, nk - 1), jnp.clip(t - sch["spec1"], 0, nk - 1))

    order_of = lambda t: jnp.where(t < sch["spec1"], 0, 1)
    last_tile = lambda t: jnp.clip(t - sch["last"], 0, nt - 1)
    const = lambda a: pl.BlockSpec(a.shape, lambda bp, lg, t: (0,) * a.ndim)
    width = 3 * c
    out = pl.pallas_call(
        functools.partial(_hyena_kernel, seq=seq),
        grid=(n_pairs, groups_per_c, sch["total"]),
        in_specs=[
            pl.BlockSpec((None, 2, HY_ROWS, HY_LANES), lambda bp, lg, t: (bp, 0, tile_of(t), p_col(t, lg))),
            pl.BlockSpec((None, 2, SUBLANES, HY_LANES),
                         lambda bp, lg, t: (bp, 0, jnp.maximum(tile_of(t) * halo_per_tile - 1, 0), p_col(t, lg))),
            pl.BlockSpec((None, 2, SUBLANES, HY_LANES),
                         lambda bp, lg, t: (bp, 0, jnp.minimum((tile_of(t) + 1) * halo_per_tile, seq // SUBLANES - 1),
                                            p_col(t, lg))),
            pl.BlockSpec((None, 2, HY_ROWS, HY_LANES), lambda bp, lg, t: (bp, 0, last_tile(t), col_gate // HY_LANES + lg)),
            pl.BlockSpec((3, HY_LANES), lambda bp, lg, t: (0, col_of(t, lg))),
            pl.BlockSpec((1, HY_LANES), lambda bp, lg, t: (0, col_of(t, lg))),
            pl.BlockSpec((None, 1, HY_LANES), lambda bp, lg, t: (order_of(t), 0, lg)),
            pl.BlockSpec((None, HY_KA, 2 * NB, HY_LANES), lambda bp, lg, t: (order_of(t), freq_of(t), 0, lg)),
            pl.BlockSpec((HY_KA, 2 * NB, 2 * NB), lambda bp, lg, t: (freq_of(t), 0, 0)),
            pl.BlockSpec((HY_KA, 2 * NB, 2 * NB), lambda bp, lg, t: (freq_of(t), 0, 0)),
            const(f1), const(f4),
        ],
        out_specs=pl.BlockSpec((None, 2, HY_ROWS, HY_LANES), lambda bp, lg, t: (bp, 0, last_tile(t), lg)),
        out_shape=jax.ShapeDtypeStruct((n_pairs, 2, seq, c), BF16),
        scratch_shapes=[
            pltpu.VMEM((HY_LANES // LANES, na_full * PITCH_T, LANES), F32),
            pltpu.VMEM((HY_LANES // LANES, na_full * PITCH_T, LANES), F32),
            pltpu.VMEM((HY_LANES // LANES, na_full * PITCH_F, LANES), F32),
        ],
        compiler_params=pltpu.CompilerParams(dimension_semantics=("parallel", "parallel", "arbitrary")),
        name="hyena_long_conv",
    )(pv, pv, pv, pv, conv_w, conv_b.reshape(1, width), filt_bias.reshape(HYENA_ORDER, 1, c), hf, g, gi, f1, f4)
    return out.reshape(bsz, seq, c)


def _out_kernel(o0_ref, o1_ref, o2_ref, l0_ref, l1_ref, l2_ref, ga_ref, yh_ref, ra_ref, rh_ref, x_ref, mod_ref,
                wpa_ref, wph_ref, wo_ref, bo_ref, lg_ref, lb_ref, out_ref, *, alpha):
    lses = [l0_ref[...], l1_ref[...], l2_ref[...]]
    m = jnp.maximum(jnp.maximum(lses[0], lses[1]), lses[2])
    es = [jnp.exp(l - m) for l in lses]
    o = es[0] * o0_ref[...].astype(F32) + es[1] * o1_ref[...].astype(F32) + es[2] * o2_ref[...].astype(F32)
    o = o / (es[0] + es[1] + es[2])
    ya = (o * _silu(ga_ref[...].astype(F32))).astype(BF16)
    merged = (_sigmoid(ra_ref[...].astype(F32)) * _mm(ya, wpa_ref[...])
              + _sigmoid(rh_ref[...].astype(F32)) * _mm(yh_ref[...], wph_ref[...]))
    out = _mm(merged.astype(BF16), wo_ref[...]) + bo_ref[...]
    res = alpha * x_ref[...] + mod_ref[2:3, :] * out
    out_ref[...] = _layer_norm(res) * lg_ref[...] + lb_ref[...]


def _out_proj(os_, lses, p, cols, yh, x, mod, wpa, wph, wo, bo, lg, lb, alpha):
    bsz, seq, d = x.shape
    da = wpa.shape[0]
    tm = OUT_TILE
    rows = lambda width, cb=0: pl.BlockSpec((None, tm, width), lambda bi, i: (bi, i, cb))
    const = lambda a: pl.BlockSpec(a.shape, lambda bi, i: (0,) * a.ndim)
    ga_col, ra_col, rh_col = cols
    return pl.pallas_call(
        functools.partial(_out_kernel, alpha=alpha),
        grid=(bsz, seq // tm),
        in_specs=[rows(da)] * 6
        + [rows(da, ga_col // da), rows(da), rows(d, ra_col // d), rows(d, rh_col // d), rows(d),
           pl.BlockSpec((None, 3, d), lambda bi, i: (bi, 0, 0)),
           const(wpa), const(wph), const(wo)]
        + [pl.BlockSpec((1, d), lambda bi, i: (0, 0))] * 3,
        out_specs=rows(d),
        out_shape=jax.ShapeDtypeStruct((bsz, seq, d), F32),
        name="merge_out_proj",
    )(*os_, *lses, p, yh, p, p, x, mod, wpa, wph, wo, bo.reshape(1, d), lg.reshape(1, d), lb.reshape(1, d))


def _layer(x, mod, w_in, b_in, conv_w, conv_b, fw1, fb1, fw2, fb2, fw3, fb3, fw4, ffreq, fbias,
           w_proj_attn, w_proj_hyena, w_out, b_out, ln_g, ln_b, alpha):
    bsz, seq, d = x.shape
    da = w_proj_attn.shape[0]
    c = w_proj_hyena.shape[0]
    n_heads = da // HEAD_DIM

    col_ga = 3 * N_GROUPS * da
    col_hy = col_ga + da
    col_gh = col_hy + 3 * c
    col_ra = col_gh + c
    col_rh = col_ra + d

    p = _inproj(x, mod, w_in.astype(BF16), b_in.reshape(1, -1))

    outs, lses = [], []
    for g, (window, dilation) in enumerate(DILATED_GROUPS):
        qkv_cols = [(i * N_GROUPS + g) * da for i in range(3)]
        o_g, lse_g = _attention(p, qkv_cols, dilation, window // (2 * dilation), n_heads)
        outs.append(o_g)
        lses.append(lse_g)

    filt = _filters(seq, c, fw1, fb1, fw2, fb2, fw3, fb3, fw4, ffreq)
    hf = _filter_spectra(filt, seq, c)
    yh = _hyena(p, col_hy, col_gh, conv_w, conv_b, fbias, hf, c)

    return _out_proj(outs, lses, p, (col_ga, col_ra, col_rh), yh, x, mod,
                     w_proj_attn.astype(BF16), w_proj_hyena.astype(BF16), w_out.astype(BF16),
                     b_out, ln_g, ln_b, alpha)


def kernel(x, c, w_ada, b_ada, w_in, b_in, conv_w, conv_b, filt_w1, filt_b1, filt_w2, filt_b2, filt_w3, filt_b3,
           filt_w4, filt_freq, filt_bias, w_proj_attn, w_proj_hyena, w_out, b_out, ln_g, ln_b):
    depth = w_ada.shape[0]
    alpha = (2 * depth) ** 0.25
    d = x.shape[-1]
    mods = _ada(c, w_ada, b_ada).reshape(depth, c.shape[0], 3, d)
    for l in range(depth):
        x = _layer(x, mods[l], w_in[l], b_in[l], conv_w[l], conv_b[l], filt_w1[l], filt_b1[l], filt_w2[l],
                   filt_b2[l], filt_w3[l], filt_b3[l], filt_w4[l], filt_freq[l], filt_bias[l],
                   w_proj_attn[l], w_proj_hyena[l], w_out[l], b_out[l], ln_g[l], ln_b[l], alpha)
    return x
```

```python
import functools
import math

import numpy as np
import jax
import jax.numpy as jnp
from jax import lax
from jax.experimental import pallas as pl
from jax.experimental.pallas import tpu as pltpu

F32 = jnp.float32
BF16 = jnp.bfloat16

HEAD_DIM = 64
DILATED_GROUPS = ((128, 1), (512, 4), (2048, 16))
N_GROUPS = len(DILATED_GROUPS)
HYENA_ORDER = 2
FILTER_EMB = 33
FILTER_WIDTH = 64
DECAY_TARGET = 1e-2
FAST_DECAY_PCT = 0.3
SLOW_DECAY_PCT = 1.5
LN_EPS = 1e-5

LANES = 128
NB = 64
PAD_EMB = 128
NEG_BIG = -1e30

ROW_TILE = 2048
COL_TILE = 1536
OUT_TILE = 512
ATTN_UNROLL = 32
ATTN_STRIDE = 4
HY_LANES = 256
HY_ROWS = 1024
HY_KA = 32
FS_ROWS = 4096
FS_KA = 64
HY_UNROLL = 32
SUBLANES = 8
PITCH_T = NB + SUBLANES
PITCH_F = 2 * NB + SUBLANES


def _mm(a, b):
    return jnp.dot(a, b, preferred_element_type=F32)


def _split(a):
    hi = a.astype(BF16)
    lo = (a - hi.astype(F32)).astype(BF16)
    return hi, lo


def _dot3(a, b):
    ah, al = _split(a)
    bh, bl = _split(b)
    return _mm(ah, bh) + _mm(ah, bl) + _mm(al, bh)


def _layer_norm(x):
    mu = jnp.mean(x, axis=-1, keepdims=True)
    xc = x - mu
    var = jnp.mean(xc * xc, axis=-1, keepdims=True)
    return xc * lax.rsqrt(var + LN_EPS)


def _sigmoid(a):
    return 0.5 * jnp.tanh(0.5 * a) + 0.5


def _silu(a):
    return a * _sigmoid(a)


@functools.lru_cache(maxsize=None)
def _dft_consts(seq):
    n = 2 * seq
    na_full = n // NB
    na_half = na_full // 2
    ka = np.arange(na_full)[:, None]
    na = np.arange(na_half)[None, :]
    ang = 2.0 * np.pi * ((ka * na) % na_full) / na_full
    c, s = np.cos(ang), np.sin(ang)
    f1 = np.block([[c, s], [-s, c]])
    f1r = np.concatenate([c, -s], axis=0)
    f4 = np.block([[c.T, -s.T], [s.T, c.T]]) / n
    kav = np.arange(na_full)[:, None, None]
    kb = np.arange(NB)[None, :, None]
    nb = np.arange(NB)[None, None, :]
    th = 2.0 * np.pi * (((nb * kav) % n) / n + ((nb * kb) % NB) / NB)
    cg, sg = np.cos(th), np.sin(th)
    g = np.concatenate([np.concatenate([cg, sg], axis=2), np.concatenate([-sg, cg], axis=2)], axis=1)
    cgt, sgt = cg.transpose(0, 2, 1), sg.transpose(0, 2, 1)
    gi = np.concatenate([np.concatenate([cgt, -sgt], axis=2), np.concatenate([sgt, cgt], axis=2)], axis=1)
    as_bf16 = lambda a: jnp.asarray(a, dtype=F32).astype(BF16)
    return as_bf16(f1), as_bf16(f1r), as_bf16(f4), as_bf16(g), as_bf16(gi)


@functools.lru_cache(maxsize=None)
def _filter_consts(seq, channels):
    t = np.linspace(0.0, 1.0, seq)[:, None]
    bands = (FILTER_EMB - 1) // 2
    w = 2.0 * np.pi * np.arange(seq)[:, None] / seq
    f = np.linspace(1e-4, bands - 1, bands)[None, :]
    pos = np.concatenate([t, np.cos(f * w), -np.sin(f * w)], axis=-1)
    pos = np.pad(pos, ((0, 0), (0, PAD_EMB - FILTER_EMB)))
    deltas = np.linspace(math.log(DECAY_TARGET) / SLOW_DECAY_PCT, math.log(DECAY_TARGET) / FAST_DECAY_PCT, channels)
    decay = np.exp(-t * np.abs(deltas)[None, :])
    return jnp.asarray(pos, dtype=F32), jnp.asarray(decay, dtype=F32)


@functools.lru_cache(maxsize=None)
def _attn_bias(n_heads, dilation, radius):
    tq, tk = 2 * radius, 4 * radius
    slopes = 2.0 ** (-8.0 * (np.arange(n_heads) + 1) / n_heads)
    i = np.arange(tq)[:, None]
    j = np.arange(tk)[None, :]
    cases = []
    for off in (0, radius, 2 * radius):
        rel = np.abs(j - off - i)
        per_head = [np.where(rel <= radius, -slopes[h] * rel * dilation, NEG_BIG) for h in range(n_heads)]
        cases.append(np.stack([np.concatenate(per_head[2 * p:2 * p + 2], axis=0) for p in range(n_heads // 2)]))
    return jnp.asarray(np.stack(cases), dtype=F32)


def _ada_kernel(c_ref, w_ref, b_ref, o_ref):
    o_ref[...] = _dot3(c_ref[...], w_ref[...]) + b_ref[...]


def _ada(c, w_ada, b_ada):
    depth, d, n3 = w_ada.shape
    bsz = c.shape[0]
    tn = d
    return pl.pallas_call(
        _ada_kernel,
        grid=(depth, n3 // tn),
        in_specs=[
            pl.BlockSpec((bsz, d), lambda l, j: (0, 0)),
            pl.BlockSpec((None, d, tn), lambda l, j: (l, 0, j)),
            pl.BlockSpec((None, 1, tn), lambda l, j: (l, 0, j)),
        ],
        out_specs=pl.BlockSpec((None, bsz, tn), lambda l, j: (l, 0, j)),
        out_shape=jax.ShapeDtypeStruct((depth, bsz, n3), F32),
        name="ada_mod",
    )(c, w_ada, b_ada.reshape(depth, 1, n3))


def _inproj_kernel(x_ref, mod_ref, w_ref, b_ref, o_ref, h_ref):
    @pl.when(pl.program_id(2) == 0)
    def _():
        hn = _layer_norm(x_ref[...])
        shift, scale = mod_ref[0:1, :], mod_ref[1:2, :]
        h_ref[...] = (hn * (1.0 + scale) + shift).astype(BF16)

    o_ref[...] = (_mm(h_ref[...], w_ref[...]) + b_ref[...]).astype(o_ref.dtype)


def _inproj(x, mod, w, b):
    bsz, seq, d = x.shape
    n = w.shape[1]
    tm = ROW_TILE
    col = lambda rows: pl.BlockSpec((rows, COL_TILE), lambda bi, i, j: (0, j))
    return pl.pallas_call(
        _inproj_kernel,
        grid=(bsz, seq // tm, n // COL_TILE),
        in_specs=[
            pl.BlockSpec((None, tm, d), lambda bi, i, j: (bi, i, 0)),
            pl.BlockSpec((None, 3, d), lambda bi, i, j: (bi, 0, 0)),
            col(d), col(1),
        ],
        out_specs=pl.BlockSpec((None, tm, COL_TILE), lambda bi, i, j: (bi, i, j)),
        out_shape=jax.ShapeDtypeStruct((bsz, seq, n), BF16),
        scratch_shapes=[pltpu.VMEM((tm, d), BF16)],
        compiler_params=pltpu.CompilerParams(dimension_semantics=("parallel", "parallel", "arbitrary")),
        name="inproj",
    )(x, mod, w, b)


def _attn_kernel(q_ref, k_ref, v_ref, bias_ref, o_ref, lse_ref, *scratch, dilation, seq, radius):
    tq, tk = 2 * radius, 4 * radius
    d = dilation
    length = seq // d
    nblk = length // tq
    first_head = lax.broadcasted_iota(jnp.int32, (tq, LANES), 1) < HEAD_DIM
    pre = max(d // ATTN_STRIDE, 1)
    inner, sub = d // pre, seq // pre
    if d > 1:
        qf, kf, vf, of, lf = scratch[:5]
        for ref, buf in ((q_ref, qf), (k_ref, kf), (v_ref, vf)):
            if pre == 1:
                buf[...] = ref[...].astype(F32)
            else:
                tmp = scratch[5]
                tmp[...] = ref[...].astype(F32)
                for r1 in range(pre):
                    buf[pl.ds(r1 * sub, sub), :] = tmp[pl.ds(r1, sub, stride=pre), :]

    def rows(r, start, n):
        if d == 1:
            return pl.ds(pl.multiple_of(start, radius), n)
        r1, r2 = r & (pre - 1), lax.shift_right_logical(r, pre.bit_length() - 1)
        return pl.ds(r1 * sub + r2 + inner * start, n, stride=inner)

    def block(i, carry):
        r, jb = lax.shift_right_logical(i, nblk.bit_length() - 1), i & (nblk - 1)
        q0 = jb * tq
        ks = jnp.clip(q0 - radius, 0, length - tk)
        case = jnp.where(jb == 0, 0, jnp.where(jb == nblk - 1, 2, 1))
        if d == 1:
            q2, k2, v2 = q_ref[rows(r, q0, tq), :], k_ref[rows(r, ks, tk), :], v_ref[rows(r, ks, tk), :]
        else:
            q2, k2, v2 = (ref[rows(r, st, n), :].astype(BF16) for ref, st, n in ((qf, q0, tq), (kf, ks, tk), (vf, ks, tk)))
        q2 = q2 * (HEAD_DIM ** -0.5)
        zero = jnp.zeros_like(q2)
        qs = jnp.concatenate([jnp.where(first_head, q2, zero), jnp.where(first_head, zero, q2)], axis=0)
        s = lax.dot_general(qs, k2, (((1,), (1,)), ((), ())), preferred_element_type=F32) + bias_ref[case]
        m = jnp.max(s, axis=1, keepdims=True)
        e = jnp.exp(s - m).astype(BF16)
        res = _mm(e, jnp.concatenate([v2, jnp.ones_like(v2)], axis=1))
        den = jnp.where(first_head, res[:tq, LANES:], res[tq:, LANES:])
        o = jnp.where(first_head, res[:tq, :LANES], res[tq:, :LANES]) / den
        lse = jnp.where(first_head, m[:tq], m[tq:]) + jnp.log(den)
        if d == 1:
            o_ref[rows(r, q0, tq), :] = o.astype(o_ref.dtype)
            lse_ref[rows(r, q0, tq), :] = lse
        else:
            of[rows(r, q0, tq), :] = o
            lf[rows(r, q0, tq), :] = lse
        return carry

    lax.fori_loop(0, d * nblk, block, 0, unroll=ATTN_UNROLL)
    if d > 1 and pre == 1:
        o_ref[...] = of[...].astype(o_ref.dtype)
        lse_ref[...] = lf[...]
    elif d > 1:
        tmp = scratch[5]
        for r1 in range(pre):
            tmp[pl.ds(r1, sub, stride=pre), :] = of[pl.ds(r1 * sub, sub), :]
            lse_ref[pl.ds(r1, sub, stride=pre), :] = lf[pl.ds(r1 * sub, sub), :]
        o_ref[...] = tmp[...].astype(o_ref.dtype)


def _attention(p, qkv_cols, dilation, radius, n_heads):
    bsz, seq, _ = p.shape
    da = n_heads * HEAD_DIM
    n_pairs = da // LANES
    bias = _attn_bias(n_heads, dilation, radius)
    kern = functools.partial(_attn_kernel, dilation=dilation, seq=seq, radius=radius)
    col = lambda c0: (lambda bi, pr: (bi, 0, c0 // LANES + pr))
    return pl.pallas_call(
        kern,
        grid=(bsz, n_pairs),
        in_specs=[pl.BlockSpec((None, seq, LANES), col(c0)) for c0 in qkv_cols]
        + [pl.BlockSpec((3, None) + bias.shape[2:], lambda bi, pr: (0, pr, 0, 0))],
        out_specs=[pl.BlockSpec((None, seq, LANES), lambda bi, pr: (bi, 0, pr))] * 2,
        out_shape=[jax.ShapeDtypeStruct((bsz, seq, da), BF16), jax.ShapeDtypeStruct((bsz, seq, da), F32)],
        scratch_shapes=[pltpu.VMEM((seq, LANES), F32)] * (0 if dilation == 1 else 5 if dilation <= ATTN_STRIDE else 6),
        name=f"attn_d{dilation}",
    )(p, p, p, bias)


def _filter_kernel(pos_ref, w1_ref, b1_ref, w2_ref, b2_ref, w3_ref, b3_ref, w4_ref, fr_ref, dec_ref, o_ref):
    fr = fr_ref[...]
    h = jnp.sin(fr * (_dot3(pos_ref[...], w1_ref[...]) + b1_ref[...]))
    h = jnp.sin(fr * (_dot3(h, w2_ref[...]) + b2_ref[...]))
    h = jnp.sin(fr * (_dot3(h, w3_ref[...]) + b3_ref[...]))
    f = _dot3(h, w4_ref[...])
    dec = dec_ref[...]
    c = dec.shape[1]
    row = pl.program_id(0) * pos_ref.shape[0] + lax.broadcasted_iota(jnp.int32, (pos_ref.shape[0], 1), 0)
    for o in range(HYENA_ORDER):
        fwd = f[:, (2 * o) * c:(2 * o + 1) * c] * dec
        bwd = jnp.where(row == 0, 0.0, f[:, (2 * o + 1) * c:(2 * o + 2) * c] * dec)
        o_ref[:, (2 * o) * c:(2 * o + 1) * c] = (fwd + bwd).astype(o_ref.dtype)
        o_ref[:, (2 * o + 1) * c:(2 * o + 2) * c] = (bwd - fwd).astype(o_ref.dtype)


def _filters(seq, c, w1, b1, w2, b2, w3, b3, w4, freq):
    pos, decay = _filter_consts(seq, c)
    tm = 512
    w1p = jnp.pad(w1, ((0, PAD_EMB - FILTER_EMB), (0, 0)))
    full = lambda a: pl.BlockSpec(a.shape, lambda i: (0,) * a.ndim)
    args = (w1p, b1.reshape(1, -1), w2, b2.reshape(1, -1), w3, b3.reshape(1, -1), w4, freq.reshape(1, -1))
    return pl.pallas_call(
        _filter_kernel,
        grid=(seq // tm,),
        in_specs=[pl.BlockSpec((tm, PAD_EMB), lambda i: (i, 0))] + [full(a) for a in args]
        + [pl.BlockSpec((tm, c), lambda i: (i, 0))],
        out_specs=pl.BlockSpec((tm, 2 * HYENA_ORDER * c), lambda i: (i, 0)),
        out_shape=jax.ShapeDtypeStruct((seq, 2 * HYENA_ORDER * c), BF16),
        name="hyena_filters",
    )(pos, *args, decay)


def _filter_spectrum_kernel(e_ref, d_ref, g_ref, f1r_ref, h_ref, v_buf, s_buf, *, seq):
    t = pl.program_id(2)
    tile_rows, n_ka = e_ref.shape[0], g_ref.shape[0]
    nt = seq // tile_rows
    na_half = seq // NB
    na_full = 2 * na_half
    groups = tile_rows // NB

    @pl.when(t < nt)
    def _():
        for s, ref in enumerate((e_ref, d_ref)):
            for j in range(groups):
                rows = pl.ds(pl.multiple_of((t * groups + j) * PITCH_T, SUBLANES), NB)
                v_buf[s, rows, :] = ref[j * NB:(j + 1) * NB, :].astype(F32)

    @pl.when(t == nt - 1)
    def _():
        def body(nb, carry):
            rhs = jnp.concatenate([v_buf[s, pl.ds(nb, na_half, stride=PITCH_T), :] for s in range(2)], axis=1)
            y = _mm(f1r_ref[...], rhs.astype(BF16))
            for ri in range(2):
                for s in range(2):
                    s_buf[s, pl.ds(ri * NB + nb, na_full, stride=PITCH_F), :] = (
                        y[ri * na_full:(ri + 1) * na_full, s * LANES:(s + 1) * LANES])
            return carry
        lax.fori_loop(0, NB, body, 0, unroll=HY_UNROLL)

    @pl.when(t >= nt)
    def _():
        def body(l, carry):
            rows = pl.ds(pl.multiple_of(((t - nt) * n_ka + l) * PITCH_F, SUBLANES), 2 * NB)
            rhs = jnp.concatenate([s_buf[s, rows, :] for s in range(2)], axis=1)
            x = _mm(g_ref[l], rhs.astype(BF16))
            h_ref[l, 0:NB, :] = x[:NB, :LANES].astype(h_ref.dtype)
            h_ref[l, NB:2 * NB, :] = (-x[NB:, LANES:]).astype(h_ref.dtype)
            return carry
        lax.fori_loop(0, n_ka, body, 0, unroll=HY_UNROLL)


def _filter_spectra(filt, seq, c):
    _, f1r, _, g, _ = _dft_consts(seq)
    na_full = 2 * seq // NB
    nt, nk = seq // FS_ROWS, na_full // FS_KA
    slabs = c // LANES
    tile_of = lambda t: jnp.minimum(t, nt - 1)
    freq_of = lambda t: jnp.clip(t - nt, 0, nk - 1)
    return pl.pallas_call(
        functools.partial(_filter_spectrum_kernel, seq=seq),
        grid=(HYENA_ORDER, slabs, nt + nk),
        in_specs=[pl.BlockSpec((FS_ROWS, LANES), lambda o, s, t: (tile_of(t), (2 * o) * slabs + s)),
                  pl.BlockSpec((FS_ROWS, LANES), lambda o, s, t: (tile_of(t), (2 * o + 1) * slabs + s)),
                  pl.BlockSpec((FS_KA, 2 * NB, 2 * NB), lambda o, s, t: (freq_of(t), 0, 0)),
                  pl.BlockSpec(f1r.shape, lambda o, s, t: (0, 0))],
        out_specs=pl.BlockSpec((None, FS_KA, 2 * NB, LANES), lambda o, s, t: (o, freq_of(t), 0, s)),
        out_shape=jax.ShapeDtypeStruct((HYENA_ORDER, na_full, 2 * NB, c), BF16),
        scratch_shapes=[pltpu.VMEM((2, (na_full // 2) * PITCH_T, LANES), F32),
                        pltpu.VMEM((2, na_full * PITCH_F, LANES), F32)],
        compiler_params=pltpu.CompilerParams(dimension_semantics=("parallel", "parallel", "arbitrary")),
        name="hyena_filter_spectrum",
    )(filt, filt, g, f1r)


def _hyena_schedule(seq):
    nt, nk = seq // HY_ROWS, (2 * seq // NB) // HY_KA
    return {"spec0": nt, "mid": nt + nk, "spec1": 2 * nt + nk, "last": 2 * nt + 2 * nk, "total": 3 * nt + 2 * nk,
            "nt": nt, "nk": nk}


def _hyena_kernel(p_ref, prev_ref, next_ref, gh_ref, cw_ref, cb_ref, fb_ref, hf_ref, g_ref, gi_ref, f1_ref, f4_ref,
                  o_ref, v_buf, y_buf, s_buf, *, seq):
    sch = _hyena_schedule(seq)
    t = pl.program_id(2)
    n_slab = HY_LANES // LANES
    na_half = seq // NB
    na_full, pair_rows = 2 * na_half, 2 * na_half
    groups = HY_ROWS // NB
    lanes = lambda s: slice(s * LANES, (s + 1) * LANES)
    row = lax.broadcasted_iota(jnp.int32, (HY_ROWS, 1), 0)

    def short_conv(tile, b):
        a = p_ref[b].astype(F32)
        top = jnp.where(tile > 0, prev_ref[b, SUBLANES - 1:SUBLANES, :].astype(F32), 0.0)
        bot = jnp.where(tile < sch["nt"] - 1, next_ref[b, 0:1, :].astype(F32), 0.0)
        up = jnp.where(row == 0, top, pltpu.roll(a, 1, 0))
        dn = jnp.where(row == HY_ROWS - 1, bot, pltpu.roll(a, HY_ROWS - 1, 0))
        return cw_ref[0:1, :] * up + cw_ref[1:2, :] * a + cw_ref[2:3, :] * dn + cb_ref[...]

    def time_rows(b, tile, j):
        return pl.ds(pl.multiple_of((b * na_half + tile * groups + j) * PITCH_T, SUBLANES), NB)

    def forward_stage1():
        def body(nb, carry):
            rhs = jnp.concatenate([v_buf[s, pl.ds(nb, pair_rows, stride=PITCH_T), :] for s in range(n_slab)], axis=1)
            y = _mm(f1_ref[...], rhs.astype(BF16))
            for ri in range(2):
                for s in range(n_slab):
                    s_buf[s, pl.ds(ri * NB + nb, na_full, stride=PITCH_F), :] = y[ri * na_full:(ri + 1) * na_full, lanes(s)]
            return carry
        lax.fori_loop(0, NB, body, 0, unroll=HY_UNROLL)

    def spectral(step):
        def body(l, carry):
            rows = pl.ds(pl.multiple_of((step * HY_KA + l) * PITCH_F, SUBLANES), 2 * NB)
            rhs = jnp.concatenate([s_buf[s, rows, :] for s in range(n_slab)], axis=1)
            x = _mm(g_ref[l], rhs.astype(BF16))
            xr, xi = x[:NB], x[NB:]
            hr, hi = hf_ref[l, 0:NB, :].astype(F32), hf_ref[l, NB:2 * NB, :].astype(F32)
            prod = jnp.concatenate([xr * hr - xi * hi, xr * hi + xi * hr], axis=0)
            tt = _mm(gi_ref[l], prod.astype(BF16))
            for s in range(n_slab):
                s_buf[s, rows, :] = tt[:, lanes(s)]
            return carry
        lax.fori_loop(0, HY_KA, body, 0, unroll=HY_UNROLL)

    def inverse_stage():
        def body(nb, carry):
            parts = [jnp.concatenate([s_buf[s, pl.ds(ri * NB + nb, na_full, stride=PITCH_F), :] for s in range(n_slab)],
                                     axis=1) for ri in range(2)]
            y = _mm(f4_ref[...], jnp.concatenate(parts, axis=0).astype(BF16))
            for s in range(n_slab):
                y_buf[s, pl.ds(nb, pair_rows, stride=PITCH_T), :] = y[:, lanes(s)]
            return carry
        lax.fori_loop(0, NB, body, 0, unroll=HY_UNROLL)

    @pl.when(t < sch["spec0"])
    def _():
        for b in range(2):
            v = short_conv(t, b)
            for j in range(groups):
                for s in range(n_slab):
                    v_buf[s, time_rows(b, t, j), :] = v[j * NB:(j + 1) * NB, lanes(s)]

    @pl.when((t >= sch["mid"]) & (t < sch["spec1"]))
    def _():
        tile = t - sch["mid"]
        for b in range(2):
            x1 = short_conv(tile, b)
            for j in range(groups):
                for s in range(n_slab):
                    rows = time_rows(b, tile, j)
                    z = x1[j * NB:(j + 1) * NB, lanes(s)] * (y_buf[s, rows, :] + fb_ref[:, lanes(s)] * v_buf[s, rows, :])
                    v_buf[s, rows, :] = z

    @pl.when((t == sch["spec0"] - 1) | (t == sch["spec1"] - 1))
    def _():
        forward_stage1()

    @pl.when((t >= sch["spec0"]) & (t < sch["mid"]))
    def _():
        spectral(t - sch["spec0"])

    @pl.when((t >= sch["spec1"]) & (t < sch["last"]))
    def _():
        spectral(t - sch["spec1"])

    @pl.when((t == sch["mid"] - 1) | (t == sch["last"] - 1))
    def _():
        inverse_stage()

    @pl.when(t >= sch["last"])
    def _():
        tile = t - sch["last"]
        for b in range(2):
            x2 = short_conv(tile, b) * _silu(gh_ref[b].astype(F32))
            for j in range(groups):
                for s in range(n_slab):
                    rows = time_rows(b, tile, j)
                    z = x2[j * NB:(j + 1) * NB, lanes(s)] * (y_buf[s, rows, :] + fb_ref[:, lanes(s)] * v_buf[s, rows, :])
                    o_ref[b, j * NB:(j + 1) * NB, lanes(s)] = z.astype(o_ref.dtype)


def _hyena(p, col_hy, col_gate, conv_w, conv_b, filt_bias, hf, c):
    bsz, seq, ncols = p.shape
    f1, _, f4, g, gi = _dft_consts(seq)
    sch = _hyena_schedule(seq)
    nt, nk = sch["nt"], sch["nk"]
    n_pairs, na_full = bsz // 2, 2 * seq // NB
    groups_per_c = c // HY_LANES
    halo_per_tile = HY_ROWS // SUBLANES
    pv = p.reshape(n_pairs, 2, seq, ncols)

    def tile_of(t):
        return jnp.where(t < sch["mid"], jnp.minimum(t, nt - 1),
                         jnp.where(t < sch["last"], jnp.clip(t - sch["mid"], 0, nt - 1), jnp.clip(t - sch["last"], 0, nt - 1)))

    def col_of(t, lg):
        return jnp.where(t < sch["mid"], 0, jnp.where(t < sch["last"], 1, 2)) * groups_per_c + lg

    p_col = lambda t, lg: col_hy // HY_LANES + col_of(t, lg)

    def freq_of(t):
        return jnp.where(t < sch["spec1"], jnp.clip(t - sch["spec0"], 0, nk - 1), jnp.clip(t - sch["spec1"], 0, nk - 1))

    order_of = lambda t: jnp.where(t < sch["spec1"], 0, 1)
    last_tile = lambda t: jnp.clip(t - sch["last"], 0, nt - 1)
    const = lambda a: pl.BlockSpec(a.shape, lambda bp, lg, t: (0,) * a.ndim)
    width = 3 * c
    out = pl.pallas_call(
        functools.partial(_hyena_kernel, seq=seq),
        grid=(n_pairs, groups_per_c, sch["total"]),
        in_specs=[
            pl.BlockSpec((None, 2, HY_ROWS, HY_LANES), lambda bp, lg, t: (bp, 0, tile_of(t), p_col(t, lg))),
            pl.BlockSpec((None, 2, SUBLANES, HY_LANES),
                         lambda bp, lg, t: (bp, 0, jnp.maximum(tile_of(t) * halo_per_tile - 1, 0), p_col(t, lg))),
            pl.BlockSpec((None, 2, SUBLANES, HY_LANES),
                         lambda bp, lg, t: (bp, 0, jnp.minimum((tile_of(t) + 1) * halo_per_tile, seq // SUBLANES - 1),
                                            p_col(t, lg))),
            pl.BlockSpec((None, 2, HY_ROWS, HY_LANES), lambda bp, lg, t: (bp, 0, last_tile(t), col_gate // HY_LANES + lg)),
            pl.BlockSpec((3, HY_LANES), lambda bp, lg, t: (0, col_of(t, lg))),
            pl.BlockSpec((1, HY_LANES), lambda bp, lg, t: (0, col_of(t, lg))),
            pl.BlockSpec((None, 1, HY_LANES), lambda bp, lg, t: (order_of(t), 0, lg)),
            pl.BlockSpec((None, HY_KA, 2 * NB, HY_LANES), lambda bp, lg, t: (order_of(t), freq_of(t), 0, lg)),
            pl.BlockSpec((HY_KA, 2 * NB, 2 * NB), lambda bp, lg, t: (freq_of(t), 0, 0)),
            pl.BlockSpec((HY_KA, 2 * NB, 2 * NB), lambda bp, lg, t: (freq_of(t), 0, 0)),
            const(f1), const(f4),
        ],
        out_specs=pl.BlockSpec((None, 2, HY_ROWS, HY_LANES), lambda bp, lg, t: (bp, 0, last_tile(t), lg)),
        out_shape=jax.ShapeDtypeStruct((n_pairs, 2, seq, c), BF16),
        scratch_shapes=[
            pltpu.VMEM((HY_LANES // LANES, na_full * PITCH_T, LANES), F32),
            pltpu.VMEM((HY_LANES // LANES, na_full * PITCH_T, LANES), F32),
            pltpu.VMEM((HY_LANES // LANES, na_full * PITCH_F, LANES), F32),
        ],
        compiler_params=pltpu.CompilerParams(dimension_semantics=("parallel", "parallel", "arbitrary")),
        name="hyena_long_conv",
    )(pv, pv, pv, pv, conv_w, conv_b.reshape(1, width), filt_bias.reshape(HYENA_ORDER, 1, c), hf, g, gi, f1, f4)
    return out.reshape(bsz, seq, c)


def _out_kernel(o0_ref, o1_ref, o2_ref, l0_ref, l1_ref, l2_ref, ga_ref, yh_ref, ra_ref, rh_ref, x_ref, mod_ref,
                wpa_ref, wph_ref, wo_ref, bo_ref, lg_ref, lb_ref, out_ref, *, alpha):
    lses = [l0_ref[...], l1_ref[...], l2_ref[...]]
    m = jnp.maximum(jnp.maximum(lses[0], lses[1]), lses[2])
    es = [jnp.exp(l - m) for l in lses]
    o = es[0] * o0_ref[...].astype(F32) + es[1] * o1_ref[...].astype(F32) + es[2] * o2_ref[...].astype(F32)
    o = o / (es[0] + es[1] + es[2])
    ya = (o * _silu(ga_ref[...].astype(F32))).astype(BF16)
    merged = (_sigmoid(ra_ref[...].astype(F32)) * _mm(ya, wpa_ref[...])
              + _sigmoid(rh_ref[...].astype(F32)) * _mm(yh_ref[...], wph_ref[...]))
    out = _mm(merged.astype(BF16), wo_ref[...]) + bo_ref[...]
    res = alpha * x_ref[...] + mod_ref[2:3, :] * out
    out_ref[...] = _layer_norm(res) * lg_ref[...] + lb_ref[...]


def _out_proj(os_, lses, p, cols, yh, x, mod, wpa, wph, wo, bo, lg, lb, alpha):
    bsz, seq, d = x.shape
    da = wpa.shape[0]
    tm = OUT_TILE
    rows = lambda width, cb=0: pl.BlockSpec((None, tm, width), lambda bi, i: (bi, i, cb))
    const = lambda a: pl.BlockSpec(a.shape, lambda bi, i: (0,) * a.ndim)
    ga_col, ra_col, rh_col = cols
    return pl.pallas_call(
        functools.partial(_out_kernel, alpha=alpha),
        grid=(bsz, seq // tm),
        in_specs=[rows(da)] * 6
        + [rows(da, ga_col // da), rows(da), rows(d, ra_col // d), rows(d, rh_col // d), rows(d),
           pl.BlockSpec((None, 3, d), lambda bi, i: (bi, 0, 0)),
           const(wpa), const(wph), const(wo)]
        + [pl.BlockSpec((1, d), lambda bi, i: (0, 0))] * 3,
        out_specs=rows(d),
        out_shape=jax.ShapeDtypeStruct((bsz, seq, d), F32),
        name="merge_out_proj",
    )(*os_, *lses, p, yh, p, p, x, mod, wpa, wph, wo, bo.reshape(1, d), lg.reshape(1, d), lb.reshape(1, d))


def _layer(x, mod, w_in, b_in, conv_w, conv_b, fw1, fb1, fw2, fb2, fw3, fb3, fw4, ffreq, fbias,
           w_proj_attn, w_proj_hyena, w_out, b_out, ln_g, ln_b, alpha):
    bsz, seq, d = x.shape
    da = w_proj_attn.shape[0]
    c = w_proj_hyena.shape[0]
    n_heads = da // HEAD_DIM

    col_ga = 3 * N_GROUPS * da
    col_hy = col_ga + da
    col_gh = col_hy + 3 * c
    col_ra = col_gh + c
    col_rh = col_ra + d

    p = _inproj(x, mod, w_in.astype(BF16), b_in.reshape(1, -1))

    outs, lses = [], []
    for g, (window, dilation) in enumerate(DILATED_GROUPS):
        qkv_cols = [(i * N_GROUPS + g) * da for i in range(3)]
        o_g, lse_g = _attention(p, qkv_cols, dilation, window // (2 * dilation), n_heads)
        outs.append(o_g)
        lses.append(lse_g)

    filt = _filters(seq, c, fw1, fb1, fw2, fb2, fw3, fb3, fw4, ffreq)
    hf = _filter_spectra(filt, seq, c)
    yh = _hyena(p, col_hy, col_gh, conv_w, conv_b, fbias, hf, c)

    return _out_proj(outs, lses, p, (col_ga, col_ra, col_rh), yh, x, mod,
                     w_proj_attn.astype(BF16), w_proj_hyena.astype(BF16), w_out.astype(BF16),
                     b_out, ln_g, ln_b, alpha)


def kernel(x, c, w_ada, b_ada, w_in, b_in, conv_w, conv_b, filt_w1, filt_b1, filt_w2, filt_b2, filt_w3, filt_b3,
           filt_w4, filt_freq, filt_bias, w_proj_attn, w_proj_hyena, w_out, b_out, ln_g, ln_b):
    depth = w_ada.shape[0]
    alpha = (2 * depth) ** 0.25
    d = x.shape[-1]
    mods = _ada(c, w_ada, b_ada).reshape(depth, c.shape[0], 3, d)
    for l in range(depth):
        x = _layer(x, mods[l], w_in[l], b_in[l], conv_w[l], conv_b[l], filt_w1[l], filt_b1[l], filt_w2[l],
                   filt_b2[l], filt_w3[l], filt_b3[l], filt_w4[l], filt_freq[l], filt_bias[l],
                   w_proj_attn[l], w_proj_hyena[l], w_out[l], b_out[l], ln_g[l], ln_b[l], alpha)
    return x
```

```python
import functools
import math

import numpy as np
import jax
import jax.numpy as jnp
from jax import lax
from jax.experimental import pallas as pl
from jax.experimental.pallas import tpu as pltpu

F32 = jnp.float32
BF16 = jnp.bfloat16

HEAD_DIM = 64
DILATED_GROUPS = ((128, 1), (512, 4), (2048, 16))
N_GROUPS = len(DILATED_GROUPS)
HYENA_ORDER = 2
FILTER_EMB = 33
FILTER_WIDTH = 64
DECAY_TARGET = 1e-2
FAST_DECAY_PCT = 0.3
SLOW_DECAY_PCT = 1.5
LN_EPS = 1e-5

LANES = 128
NB = 64
PAD_EMB = 128
NEG_BIG = -1e30

ROW_TILE = 2048
COL_TILE = 1536
OUT_TILE = 512
ATTN_UNROLL = 32
ATTN_STRIDE = 4
HY_LANES = 256
HY_ROWS = 2048
HY_KA = 32
FS_ROWS = 4096
FS_KA = 64
HY_UNROLL = 32
SUBLANES = 8
PITCH_T = NB + SUBLANES
PITCH_F = 2 * NB + SUBLANES


def _mm(a, b):
    return jnp.dot(a, b, preferred_element_type=F32)


def _split(a):
    hi = a.astype(BF16)
    lo = (a - hi.astype(F32)).astype(BF16)
    return hi, lo


def _dot3(a, b):
    ah, al = _split(a)
    bh, bl = _split(b)
    return _mm(ah, bh) + _mm(ah, bl) + _mm(al, bh)


def _layer_norm(x):
    mu = jnp.mean(x, axis=-1, keepdims=True)
    xc = x - mu
    var = jnp.mean(xc * xc, axis=-1, keepdims=True)
    return xc * lax.rsqrt(var + LN_EPS)


def _sigmoid(a):
    return 0.5 * jnp.tanh(0.5 * a) + 0.5


def _silu(a):
    return a * _sigmoid(a)


@functools.lru_cache(maxsize=None)
def _dft_consts(seq):
    n = 2 * seq
    na_full = n // NB
    na_half = na_full // 2
    ka = np.arange(na_full)[:, None]
    na = np.arange(na_half)[None, :]
    ang = 2.0 * np.pi * ((ka * na) % na_full) / na_full
    c, s = np.cos(ang), np.sin(ang)
    f1 = np.block([[c, s], [-s, c]])
    f1r = np.concatenate([c, -s], axis=0)
    f4 = np.block([[c.T, -s.T], [s.T, c.T]]) / n
    kav = np.arange(na_full)[:, None, None]
    kb = np.arange(NB)[None, :, None]
    nb = np.arange(NB)[None, None, :]
    th = 2.0 * np.pi * (((nb * kav) % n) / n + ((nb * kb) % NB) / NB)
    cg, sg = np.cos(th), np.sin(th)
    g = np.concatenate([np.concatenate([cg, sg], axis=2), np.concatenate([-sg, cg], axis=2)], axis=1)
    cgt, sgt = cg.transpose(0, 2, 1), sg.transpose(0, 2, 1)
    gi = np.concatenate([np.concatenate([cgt, -sgt], axis=2), np.concatenate([sgt, cgt], axis=2)], axis=1)
    as_bf16 = lambda a: jnp.asarray(a, dtype=F32).astype(BF16)
    return as_bf16(f1), as_bf16(f1r), as_bf16(f4), as_bf16(g), as_bf16(gi)


@functools.lru_cache(maxsize=None)
def _filter_consts(seq, channels):
    t = np.linspace(0.0, 1.0, seq)[:, None]
    bands = (FILTER_EMB - 1) // 2
    w = 2.0 * np.pi * np.arange(seq)[:, None] / seq
    f = np.linspace(1e-4, bands - 1, bands)[None, :]
    pos = np.concatenate([t, np.cos(f * w), -np.sin(f * w)], axis=-1)
    pos = np.pad(pos, ((0, 0), (0, PAD_EMB - FILTER_EMB)))
    deltas = np.linspace(math.log(DECAY_TARGET) / SLOW_DECAY_PCT, math.log(DECAY_TARGET) / FAST_DECAY_PCT, channels)
    decay = np.exp(-t * np.abs(deltas)[None, :])
    return jnp.asarray(pos, dtype=F32), jnp.asarray(decay, dtype=F32)


@functools.lru_cache(maxsize=None)
def _attn_bias(n_heads, dilation, radius):
    tq, tk = 2 * radius, 4 * radius
    slopes = 2.0 ** (-8.0 * (np.arange(n_heads) + 1) / n_heads)
    i = np.arange(tq)[:, None]
    j = np.arange(tk)[None, :]
    cases = []
    for off in (0, radius, 2 * radius):
        rel = np.abs(j - off - i)
        per_head = [np.where(rel <= radius, -slopes[h] * rel * dilation, NEG_BIG) for h in range(n_heads)]
        cases.append(np.stack([np.concatenate(per_head[2 * p:2 * p + 2], axis=0) for p in range(n_heads // 2)]))
    return jnp.asarray(np.stack(cases), dtype=F32)


def _ada_kernel(c_ref, w_ref, b_ref, o_ref):
    o_ref[...] = _dot3(c_ref[...], w_ref[...]) + b_ref[...]


def _ada(c, w_ada, b_ada):
    depth, d, n3 = w_ada.shape
    bsz = c.shape[0]
    tn = d
    return pl.pallas_call(
        _ada_kernel,
        grid=(depth, n3 // tn),
        in_specs=[
            pl.BlockSpec((bsz, d), lambda l, j: (0, 0)),
            pl.BlockSpec((None, d, tn), lambda l, j: (l, 0, j)),
            pl.BlockSpec((None, 1, tn), lambda l, j: (l, 0, j)),
        ],
        out_specs=pl.BlockSpec((None, bsz, tn), lambda l, j: (l, 0, j)),
        out_shape=jax.ShapeDtypeStruct((depth, bsz, n3), F32),
        name="ada_mod",
    )(c, w_ada, b_ada.reshape(depth, 1, n3))


def _inproj_kernel(x_ref, mod_ref, w_ref, b_ref, o_ref, h_ref):
    @pl.when(pl.program_id(2) == 0)
    def _():
        hn = _layer_norm(x_ref[...])
        shift, scale = mod_ref[0:1, :], mod_ref[1:2, :]
        h_ref[...] = (hn * (1.0 + scale) + shift).astype(BF16)

    o_ref[...] = (_mm(h_ref[...], w_ref[...]) + b_ref[...]).astype(o_ref.dtype)


def _inproj(x, mod, w, b, layer):
    bsz, seq, d = x.shape
    n = w.shape[-1]
    tm = ROW_TILE
    col = lambda rows: pl.BlockSpec((None, rows, COL_TILE), lambda bi, i, j: (layer, 0, j))
    return pl.pallas_call(
        _inproj_kernel,
        grid=(bsz, seq // tm, n // COL_TILE),
        in_specs=[
            pl.BlockSpec((None, tm, d), lambda bi, i, j: (bi, i, 0)),
            pl.BlockSpec((None, 3, d), lambda bi, i, j: (bi, 0, 0)),
            col(d), col(1),
        ],
        out_specs=pl.BlockSpec((None, tm, COL_TILE), lambda bi, i, j: (bi, i, j)),
        out_shape=jax.ShapeDtypeStruct((bsz, seq, n), BF16),
        scratch_shapes=[pltpu.VMEM((tm, d), BF16)],
        compiler_params=pltpu.CompilerParams(dimension_semantics=("parallel", "parallel", "arbitrary")),
        name="inproj",
    )(x, mod, w, b)


def _attn_kernel(q_ref, k_ref, v_ref, bias_ref, o_ref, lse_ref, *scratch, dilation, seq, radius):
    tq, tk = 2 * radius, 4 * radius
    d = dilation
    length = seq // d
    nblk = length // tq
    first_head = lax.broadcasted_iota(jnp.int32, (tq, LANES), 1) < HEAD_DIM
    pre = max(d // ATTN_STRIDE, 1)
    inner, sub = d // pre, seq // pre
    if d > 1:
        qf, kf, vf, of, lf = scratch[:5]
        for ref, buf in ((q_ref, qf), (k_ref, kf), (v_ref, vf)):
            if pre == 1:
                buf[...] = ref[...].astype(F32)
            else:
                tmp = scratch[5]
                tmp[...] = ref[...].astype(F32)
                for r1 in range(pre):
                    buf[pl.ds(r1 * sub, sub), :] = tmp[pl.ds(r1, sub, stride=pre), :]

    def rows(r, start, n):
        if d == 1:
            return pl.ds(pl.multiple_of(start, radius), n)
        r1, r2 = r & (pre - 1), lax.shift_right_logical(r, pre.bit_length() - 1)
        return pl.ds(r1 * sub + r2 + inner * start, n, stride=inner)

    def block(i, carry):
        r, jb = lax.shift_right_logical(i, nblk.bit_length() - 1), i & (nblk - 1)
        q0 = jb * tq
        ks = jnp.clip(q0 - radius, 0, length - tk)
        case = jnp.where(jb == 0, 0, jnp.where(jb == nblk - 1, 2, 1))
        if d == 1:
            q2, k2, v2 = q_ref[rows(r, q0, tq), :], k_ref[rows(r, ks, tk), :], v_ref[rows(r, ks, tk), :]
        else:
            q2, k2, v2 = (ref[rows(r, st, n), :].astype(BF16) for ref, st, n in ((qf, q0, tq), (kf, ks, tk), (vf, ks, tk)))
        q2 = q2 * (HEAD_DIM ** -0.5)
        zero = jnp.zeros_like(q2)
        qs = jnp.concatenate([jnp.where(first_head, q2, zero), jnp.where(first_head, zero, q2)], axis=0)
        s = lax.dot_general(qs, k2, (((1,), (1,)), ((), ())), preferred_element_type=F32) + bias_ref[case]
        m = jnp.max(s, axis=1, keepdims=True)
        e = jnp.exp(s - m).astype(BF16)
        res = _mm(e, jnp.concatenate([v2, jnp.ones_like(v2)], axis=1))
        den = jnp.where(first_head, res[:tq, LANES:], res[tq:, LANES:])
        o = jnp.where(first_head, res[:tq, :LANES], res[tq:, :LANES]) / den
        lse = jnp.where(first_head, m[:tq], m[tq:]) + jnp.log(den)
        if d == 1:
            o_ref[rows(r, q0, tq), :] = o.astype(o_ref.dtype)
            lse_ref[rows(r, q0, tq), :] = lse
        else:
            of[rows(r, q0, tq), :] = o
            lf[rows(r, q0, tq), :] = lse
        return carry

    lax.fori_loop(0, d * nblk, block, 0, unroll=ATTN_UNROLL)
    if d > 1 and pre == 1:
        o_ref[...] = of[...].astype(o_ref.dtype)
        lse_ref[...] = lf[...]
    elif d > 1:
        tmp = scratch[5]
        for r1 in range(pre):
            tmp[pl.ds(r1, sub, stride=pre), :] = of[pl.ds(r1 * sub, sub), :]
            lse_ref[pl.ds(r1, sub, stride=pre), :] = lf[pl.ds(r1 * sub, sub), :]
        o_ref[...] = tmp[...].astype(o_ref.dtype)


def _attention(p, qkv_cols, dilation, radius, n_heads):
    bsz, seq, _ = p.shape
    da = n_heads * HEAD_DIM
    n_pairs = da // LANES
    bias = _attn_bias(n_heads, dilation, radius)
    kern = functools.partial(_attn_kernel, dilation=dilation, seq=seq, radius=radius)
    col = lambda c0: (lambda bi, pr: (bi, 0, c0 // LANES + pr))
    return pl.pallas_call(
        kern,
        grid=(bsz, n_pairs),
        in_specs=[pl.BlockSpec((None, seq, LANES), col(c0)) for c0 in qkv_cols]
        + [pl.BlockSpec((3, None) + bias.shape[2:], lambda bi, pr: (0, pr, 0, 0))],
        out_specs=[pl.BlockSpec((None, seq, LANES), lambda bi, pr: (bi, 0, pr))] * 2,
        out_shape=[jax.ShapeDtypeStruct((bsz, seq, da), BF16), jax.ShapeDtypeStruct((bsz, seq, da), F32)],
        scratch_shapes=[pltpu.VMEM((seq, LANES), F32)] * (0 if dilation == 1 else 5 if dilation <= ATTN_STRIDE else 6),
        name=f"attn_d{dilation}",
    )(p, p, p, bias)


def _filter_kernel(pos_ref, w1_ref, b1_ref, w2_ref, b2_ref, w3_ref, b3_ref, w4_ref, fr_ref, dec_ref, o_ref):
    fr = fr_ref[...]
    h = jnp.sin(fr * (_dot3(pos_ref[...], w1_ref[...]) + b1_ref[...]))
    h = jnp.sin(fr * (_dot3(h, w2_ref[...]) + b2_ref[...]))
    h = jnp.sin(fr * (_dot3(h, w3_ref[...]) + b3_ref[...]))
    f = _dot3(h, w4_ref[...])
    dec = dec_ref[...]
    c = dec.shape[1]
    row = pl.program_id(0) * pos_ref.shape[0] + lax.broadcasted_iota(jnp.int32, (pos_ref.shape[0], 1), 0)
    for o in range(HYENA_ORDER):
        fwd = f[:, (2 * o) * c:(2 * o + 1) * c] * dec
        bwd = jnp.where(row == 0, 0.0, f[:, (2 * o + 1) * c:(2 * o + 2) * c] * dec)
        o_ref[:, (2 * o) * c:(2 * o + 1) * c] = (fwd + bwd).astype(o_ref.dtype)
        o_ref[:, (2 * o + 1) * c:(2 * o + 2) * c] = (bwd - fwd).astype(o_ref.dtype)


def _filters(seq, c, w1, b1, w2, b2, w3, b3, w4, freq):
    pos, decay = _filter_consts(seq, c)
    tm = 512
    w1p = jnp.pad(w1, ((0, PAD_EMB - FILTER_EMB), (0, 0)))
    full = lambda a: pl.BlockSpec(a.shape, lambda i: (0,) * a.ndim)
    args = (w1p, b1.reshape(1, -1), w2, b2.reshape(1, -1), w3, b3.reshape(1, -1), w4, freq.reshape(1, -1))
    return pl.pallas_call(
        _filter_kernel,
        grid=(seq // tm,),
        in_specs=[pl.BlockSpec((tm, PAD_EMB), lambda i: (i, 0))] + [full(a) for a in args]
        + [pl.BlockSpec((tm, c), lambda i: (i, 0))],
        out_specs=pl.BlockSpec((tm, 2 * HYENA_ORDER * c), lambda i: (i, 0)),
        out_shape=jax.ShapeDtypeStruct((seq, 2 * HYENA_ORDER * c), BF16),
        name="hyena_filters",
    )(pos, *args, decay)


def _filter_spectrum_kernel(e_ref, d_ref, g_ref, f1r_ref, h_ref, v_buf, s_buf, *, seq):
    t = pl.program_id(2)
    tile_rows, n_ka = e_ref.shape[0], g_ref.shape[0]
    nt = seq // tile_rows
    na_half = seq // NB
    na_full = 2 * na_half
    groups = tile_rows // NB

    @pl.when(t < nt)
    def _():
        for s, ref in enumerate((e_ref, d_ref)):
            for j in range(groups):
                rows = pl.ds(pl.multiple_of((t * groups + j) * PITCH_T, SUBLANES), NB)
                v_buf[s, rows, :] = ref[j * NB:(j + 1) * NB, :].astype(F32)

    @pl.when(t == nt - 1)
    def _():
        def body(nb, carry):
            rhs = jnp.concatenate([v_buf[s, pl.ds(nb, na_half, stride=PITCH_T), :] for s in range(2)], axis=1)
            y = _mm(f1r_ref[...], rhs.astype(BF16))
            for ri in range(2):
                for s in range(2):
                    s_buf[s, pl.ds(ri * NB + nb, na_full, stride=PITCH_F), :] = (
                        y[ri * na_full:(ri + 1) * na_full, s * LANES:(s + 1) * LANES])
            return carry
        lax.fori_loop(0, NB, body, 0, unroll=HY_UNROLL)

    @pl.when(t >= nt)
    def _():
        def body(l, carry):
            rows = pl.ds(pl.multiple_of(((t - nt) * n_ka + l) * PITCH_F, SUBLANES), 2 * NB)
            rhs = jnp.concatenate([s_buf[s, rows, :] for s in range(2)], axis=1)
            x = _mm(g_ref[l], rhs.astype(BF16))
            h_ref[l, 0:NB, :] = x[:NB, :LANES].astype(h_ref.dtype)
            h_ref[l, NB:2 * NB, :] = (-x[NB:, LANES:]).astype(h_ref.dtype)
            return carry
        lax.fori_loop(0, n_ka, body, 0, unroll=HY_UNROLL)


def _filter_spectra(filt, seq, c):
    _, f1r, _, g, _ = _dft_consts(seq)
    na_full = 2 * seq // NB
    nt, nk = seq // FS_ROWS, na_full // FS_KA
    slabs = c // LANES
    tile_of = lambda t: jnp.minimum(t, nt - 1)
    freq_of = lambda t: jnp.clip(t - nt, 0, nk - 1)
    return pl.pallas_call(
        functools.partial(_filter_spectrum_kernel, seq=seq),
        grid=(HYENA_ORDER, slabs, nt + nk),
        in_specs=[pl.BlockSpec((FS_ROWS, LANES), lambda o, s, t: (tile_of(t), (2 * o) * slabs + s)),
                  pl.BlockSpec((FS_ROWS, LANES), lambda o, s, t: (tile_of(t), (2 * o + 1) * slabs + s)),
                  pl.BlockSpec((FS_KA, 2 * NB, 2 * NB), lambda o, s, t: (freq_of(t), 0, 0)),
                  pl.BlockSpec(f1r.shape, lambda o, s, t: (0, 0))],
        out_specs=pl.BlockSpec((None, FS_KA, 2 * NB, LANES), lambda o, s, t: (o, freq_of(t), 0, s)),
        out_shape=jax.ShapeDtypeStruct((HYENA_ORDER, na_full, 2 * NB, c), BF16),
        scratch_shapes=[pltpu.VMEM((2, (na_full // 2) * PITCH_T, LANES), F32),
                        pltpu.VMEM((2, na_full * PITCH_F, LANES), F32)],
        compiler_params=pltpu.CompilerParams(dimension_semantics=("parallel", "parallel", "arbitrary")),
        name="hyena_filter_spectrum",
    )(filt, filt, g, f1r)


def _hyena_schedule(seq):
    nt, nk = seq // HY_ROWS, (2 * seq // NB) // HY_KA
    return {"spec0": nt, "mid": nt + nk, "spec1": 2 * nt + nk, "last": 2 * nt + 2 * nk, "total": 3 * nt + 2 * nk,
            "nt": nt, "nk": nk}


def _hyena_kernel(p_ref, prev_ref, next_ref, gh_ref, cw_ref, cb_ref, fb_ref, hf_ref, g_ref, gi_ref, f1_ref, f4_ref,
                  o_ref, v_buf, y_buf, s_buf, *, seq):
    sch = _hyena_schedule(seq)
    t = pl.program_id(2)
    n_slab = HY_LANES // LANES
    na_half = seq // NB
    na_full, pair_rows = 2 * na_half, 2 * na_half
    groups = HY_ROWS // NB
    lanes = lambda s: slice(s * LANES, (s + 1) * LANES)
    row = lax.broadcasted_iota(jnp.int32, (HY_ROWS, 1), 0)

    def short_conv(tile, b):
        a = p_ref[b].astype(F32)
        top = jnp.where(tile > 0, prev_ref[b, SUBLANES - 1:SUBLANES, :].astype(F32), 0.0)
        bot = jnp.where(tile < sch["nt"] - 1, next_ref[b, 0:1, :].astype(F32), 0.0)
        up = jnp.where(row == 0, top, pltpu.roll(a, 1, 0))
        dn = jnp.where(row == HY_ROWS - 1, bot, pltpu.roll(a, HY_ROWS - 1, 0))
        return cw_ref[0:1, :] * up + cw_ref[1:2, :] * a + cw_ref[2:3, :] * dn + cb_ref[...]

    def time_rows(b, tile, j):
        return pl.ds(pl.multiple_of((b * na_half + tile * groups + j) * PITCH_T, SUBLANES), NB)

    def forward_stage1():
        def body(nb, carry):
            rhs = jnp.concatenate([v_buf[s, pl.ds(nb, pair_rows, stride=PITCH_T), :] for s in range(n_slab)], axis=1)
            y = _mm(f1_ref[...], rhs.astype(BF16))
            for ri in range(2):
                for s in range(n_slab):
                    s_buf[s, pl.ds(ri * NB + nb, na_full, stride=PITCH_F), :] = y[ri * na_full:(ri + 1) * na_full, lanes(s)]
            return carry
        lax.fori_loop(0, NB, body, 0, unroll=HY_UNROLL)

    def spectral(step):
        def body(l, carry):
            rows = pl.ds(pl.multiple_of((step * HY_KA + l) * PITCH_F, SUBLANES), 2 * NB)
            rhs = jnp.concatenate([s_buf[s, rows, :] for s in range(n_slab)], axis=1)
            x = _mm(g_ref[l], rhs.astype(BF16))
            xr, xi = x[:NB], x[NB:]
            hr, hi = hf_ref[l, 0:NB, :].astype(F32), hf_ref[l, NB:2 * NB, :].astype(F32)
            prod = jnp.concatenate([xr * hr - xi * hi, xr * hi + xi * hr], axis=0)
            tt = _mm(gi_ref[l], prod.astype(BF16))
            for s in range(n_slab):
                s_buf[s, rows, :] = tt[:, lanes(s)]
            return carry
        lax.fori_loop(0, HY_KA, body, 0, unroll=HY_UNROLL)

    def inverse_stage():
        def body(nb, carry):
            parts = [jnp.concatenate([s_buf[s, pl.ds(ri * NB + nb, na_full, stride=PITCH_F), :] for s in range(n_slab)],
                                     axis=1) for ri in range(2)]
            y = _mm(f4_ref[...], jnp.concatenate(parts, axis=0).astype(BF16))
            for s in range(n_slab):
                y_buf[s, pl.ds(nb, pair_rows, stride=PITCH_T), :] = y[:, lanes(s)]
            return carry
        lax.fori_loop(0, NB, body, 0, unroll=HY_UNROLL)

    @pl.when(t < sch["spec0"])
    def _():
        for b in range(2):
            v = short_conv(t, b)
            for j in range(groups):
                for s in range(n_slab):
                    v_buf[s, time_rows(b, t, j), :] = v[j * NB:(j + 1) * NB, lanes(s)]

    @pl.when((t >= sch["mid"]) & (t < sch["spec1"]))
    def _():
        tile = t - sch["mid"]
        for b in range(2):
            x1 = short_conv(tile, b)
            for j in range(groups):
                for s in range(n_slab):
                    rows = time_rows(b, tile, j)
                    z = x1[j * NB:(j + 1) * NB, lanes(s)] * (y_buf[s, rows, :] + fb_ref[:, lanes(s)] * v_buf[s, rows, :])
                    v_buf[s, rows, :] = z

    @pl.when((t == sch["spec0"] - 1) | (t == sch["spec1"] - 1))
    def _():
        forward_stage1()

    @pl.when((t >= sch["spec0"]) & (t < sch["mid"]))
    def _():
        spectral(t - sch["spec0"])

    @pl.when((t >= sch["spec1"]) & (t < sch["last"]))
    def _():
        spectral(t - sch["spec1"])

    @pl.when((t == sch["mid"] - 1) | (t == sch["last"] - 1))
    def _():
        inverse_stage()

    @pl.when(t >= sch["last"])
    def _():
        tile = t - sch["last"]
        for b in range(2):
            x2 = short_conv(tile, b) * _silu(gh_ref[b].astype(F32))
            for j in range(groups):
                for s in range(n_slab):
                    rows = time_rows(b, tile, j)
                    z = x2[j * NB:(j + 1) * NB, lanes(s)] * (y_buf[s, rows, :] + fb_ref[:, lanes(s)] * v_buf[s, rows, :])
                    o_ref[b, j * NB:(j + 1) * NB, lanes(s)] = z.astype(o_ref.dtype)


def _hyena(p, col_hy, col_gate, conv_w, conv_b, filt_bias, hf, c):
    bsz, seq, ncols = p.shape
    f1, _, f4, g, gi = _dft_consts(seq)
    sch = _hyena_schedule(seq)
    nt, nk = sch["nt"], sch["nk"]
    n_pairs, na_full = bsz // 2, 2 * seq // NB
    groups_per_c = c // HY_LANES
    halo_per_tile = HY_ROWS // SUBLANES
    pv = p.reshape(n_pairs, 2, seq, ncols)

    def tile_of(t):
        return jnp.where(t < sch["mid"], jnp.minimum(t, nt - 1),
                         jnp.where(t < sch["last"], jnp.clip(t - sch["mid"], 0, nt - 1), jnp.clip(t - sch["last"], 0, nt - 1)))

    def col_of(t, lg):
        return jnp.where(t < sch["mid"], 0, jnp.where(t < sch["last"], 1, 2)) * groups_per_c + lg

    p_col = lambda t, lg: col_hy // HY_LANES + col_of(t, lg)

    def freq_of(t):
        return jnp.where(t < sch["spec1"], jnp.clip(t - sch["spec0"], 0, nk - 1), jnp.clip(t - sch["spec1"], 0, nk - 1))

    order_of = lambda t: jnp.where(t < sch["spec1"], 0, 1)
    last_tile = lambda t: jnp.clip(t - sch["last"], 0, nt - 1)
    const = lambda a: pl.BlockSpec(a.shape, lambda bp, lg, t: (0,) * a.ndim)
    width = 3 * c
    out = pl.pallas_call(
        functools.partial(_hyena_kernel, seq=seq),
        grid=(n_pairs, groups_per_c, sch["total"]),
        in_specs=[
            pl.BlockSpec((None, 2, HY_ROWS, HY_LANES), lambda bp, lg, t: (bp, 0, tile_of(t), p_col(t, lg))),
            pl.BlockSpec((None, 2, SUBLANES, HY_LANES),
                         lambda bp, lg, t: (bp, 0, jnp.maximum(tile_of(t) * halo_per_tile - 1, 0), p_col(t, lg))),
            pl.BlockSpec((None, 2, SUBLANES, HY_LANES),
                         lambda bp, lg, t: (bp, 0, jnp.minimum((tile_of(t) + 1) * halo_per_tile, seq // SUBLANES - 1),
                                            p_col(t, lg))),
            pl.BlockSpec((None, 2, HY_ROWS, HY_LANES), lambda bp, lg, t: (bp, 0, last_tile(t), col_gate // HY_LANES + lg)),
            pl.BlockSpec((3, HY_LANES), lambda bp, lg, t: (0, col_of(t, lg))),
            pl.BlockSpec((1, HY_LANES), lambda bp, lg, t: (0, col_of(t, lg))),
            pl.BlockSpec((None, 1, HY_LANES), lambda bp, lg, t: (order_of(t), 0, lg)),
            pl.BlockSpec((None, HY_KA, 2 * NB, HY_LANES), lambda bp, lg, t: (order_of(t), freq_of(t), 0, lg)),
            pl.BlockSpec((HY_KA, 2 * NB, 2 * NB), lambda bp, lg, t: (freq_of(t), 0, 0)),
            pl.BlockSpec((HY_KA, 2 * NB, 2 * NB), lambda bp, lg, t: (freq_of(t), 0, 0)),
            const(f1), const(f4),
        ],
        out_specs=pl.BlockSpec((None, 2, HY_ROWS, HY_LANES), lambda bp, lg, t: (bp, 0, last_tile(t), lg)),
        out_shape=jax.ShapeDtypeStruct((n_pairs, 2, seq, c), BF16),
        scratch_shapes=[
            pltpu.VMEM((HY_LANES // LANES, na_full * PITCH_T, LANES), F32),
            pltpu.VMEM((HY_LANES // LANES, na_full * PITCH_T, LANES), F32),
            pltpu.VMEM((HY_LANES // LANES, na_full * PITCH_F, LANES), F32),
        ],
        compiler_params=pltpu.CompilerParams(dimension_semantics=("parallel", "parallel", "arbitrary")),
        name="hyena_long_conv",
    )(pv, pv, pv, pv, conv_w, conv_b.reshape(1, width), filt_bias.reshape(HYENA_ORDER, 1, c), hf, g, gi, f1, f4)
    return out.reshape(bsz, seq, c)


def _out_kernel(o0_ref, o1_ref, o2_ref, l0_ref, l1_ref, l2_ref, ga_ref, yh_ref, ra_ref, rh_ref, x_ref, mod_ref,
                wpa_ref, wph_ref, wo_ref, bo_ref, lg_ref, lb_ref, out_ref, *, alpha):
    lses = [l0_ref[...], l1_ref[...], l2_ref[...]]
    m = jnp.maximum(jnp.maximum(lses[0], lses[1]), lses[2])
    es = [jnp.exp(l - m) for l in lses]
    o = es[0] * o0_ref[...].astype(F32) + es[1] * o1_ref[...].astype(F32) + es[2] * o2_ref[...].astype(F32)
    o = o / (es[0] + es[1] + es[2])
    ya = (o * _silu(ga_ref[...].astype(F32))).astype(BF16)
    merged = (_sigmoid(ra_ref[...].astype(F32)) * _mm(ya, wpa_ref[...])
              + _sigmoid(rh_ref[...].astype(F32)) * _mm(yh_ref[...], wph_ref[...]))
    out = _mm(merged.astype(BF16), wo_ref[...]) + bo_ref[...]
    res = alpha * x_ref[...] + mod_ref[2:3, :] * out
    out_ref[...] = _layer_norm(res) * lg_ref[...] + lb_ref[...]


def _out_proj(os_, lses, p, cols, yh, x, mod, wpa, wph, wo, bo, lg, lb, alpha):
    bsz, seq, d = x.shape
    da = wpa.shape[0]
    tm = OUT_TILE
    rows = lambda width, cb=0: pl.BlockSpec((None, tm, width), lambda bi, i: (bi, i, cb))
    const = lambda a: pl.BlockSpec(a.shape, lambda bi, i: (0,) * a.ndim)
    ga_col, ra_col, rh_col = cols
    return pl.pallas_call(
        functools.partial(_out_kernel, alpha=alpha),
        grid=(bsz, seq // tm),
        in_specs=[rows(da)] * 6
        + [rows(da, ga_col // da), rows(da), rows(d, ra_col // d), rows(d, rh_col // d), rows(d),
           pl.BlockSpec((None, 3, d), lambda bi, i: (bi, 0, 0)),
           const(wpa), const(wph), const(wo)]
        + [pl.BlockSpec((1, d), lambda bi, i: (0, 0))] * 3,
        out_specs=rows(d),
        out_shape=jax.ShapeDtypeStruct((bsz, seq, d), F32),
        name="merge_out_proj",
    )(*os_, *lses, p, yh, p, p, x, mod, wpa, wph, wo, bo.reshape(1, d), lg.reshape(1, d), lb.reshape(1, d))


def _layer(x, mod, layer, w_in_all, b_in_all, conv_w, conv_b, fw1, fb1, fw2, fb2, fw3, fb3, fw4, ffreq, fbias,
           w_proj_attn, w_proj_hyena, w_out, b_out, ln_g, ln_b, alpha):
    bsz, seq, d = x.shape
    da = w_proj_attn.shape[0]
    c = w_proj_hyena.shape[0]
    n_heads = da // HEAD_DIM

    col_ga = 3 * N_GROUPS * da
    col_hy = col_ga + da
    col_gh = col_hy + 3 * c
    col_ra = col_gh + c
    col_rh = col_ra + d

    p = _inproj(x, mod, w_in_all, b_in_all, layer)

    outs, lses = [], []
    for g, (window, dilation) in enumerate(DILATED_GROUPS):
        qkv_cols = [(i * N_GROUPS + g) * da for i in range(3)]
        o_g, lse_g = _attention(p, qkv_cols, dilation, window // (2 * dilation), n_heads)
        outs.append(o_g)
        lses.append(lse_g)

    filt = _filters(seq, c, fw1, fb1, fw2, fb2, fw3, fb3, fw4, ffreq)
    hf = _filter_spectra(filt, seq, c)
    yh = _hyena(p, col_hy, col_gh, conv_w, conv_b, fbias, hf, c)

    return _out_proj(outs, lses, p, (col_ga, col_ra, col_rh), yh, x, mod,
                     w_proj_attn.astype(BF16), w_proj_hyena.astype(BF16), w_out.astype(BF16),
                     b_out, ln_g, ln_b, alpha)


def kernel(x, c, w_ada, b_ada, w_in, b_in, conv_w, conv_b, filt_w1, filt_b1, filt_w2, filt_b2, filt_w3, filt_b3,
           filt_w4, filt_freq, filt_bias, w_proj_attn, w_proj_hyena, w_out, b_out, ln_g, ln_b):
    depth = w_ada.shape[0]
    alpha = (2 * depth) ** 0.25
    d = x.shape[-1]
    mods = _ada(c, w_ada, b_ada).reshape(depth, c.shape[0], 3, d)
    w_in_bf16, b_in_rows = w_in.astype(BF16), b_in.reshape(depth, 1, -1)
    for l in range(depth):
        x = _layer(x, mods[l], l, w_in_bf16, b_in_rows, conv_w[l], conv_b[l], filt_w1[l], filt_b1[l], filt_w2[l],
                   filt_b2[l], filt_w3[l], filt_b3[l], filt_w4[l], filt_freq[l], filt_bias[l],
                   w_proj_attn[l], w_proj_hyena[l], w_out[l], b_out[l], ln_g[l], ln_b[l], alpha)
    return x
```

```python
import functools
import math

import numpy as np
import jax
import jax.numpy as jnp
from jax import lax
from jax.experimental import pallas as pl
from jax.experimental.pallas import tpu as pltpu

F32 = jnp.float32
BF16 = jnp.bfloat16

HEAD_DIM = 64
DILATED_GROUPS = ((128, 1), (512, 4), (2048, 16))
N_GROUPS = len(DILATED_GROUPS)
HYENA_ORDER = 2
FILTER_EMB = 33
FILTER_WIDTH = 64
DECAY_TARGET = 1e-2
FAST_DECAY_PCT = 0.3
SLOW_DECAY_PCT = 1.5
LN_EPS = 1e-5

LANES = 128
NB = 64
PAD_EMB = 128
NEG_BIG = -1e30

ROW_TILE = 2048
COL_TILE = 2304
OUT_TILE = 512
ATTN_UNROLL = 32
ATTN_STRIDE = 4
HY_LANES = 256
HY_ROWS = 2048
HY_KA = 32
FS_ROWS = 4096
FS_KA = 64
HY_UNROLL = 32
SUBLANES = 8
PITCH_T = NB + SUBLANES
PITCH_F = 2 * NB + SUBLANES


def _mm(a, b):
    return jnp.dot(a, b, preferred_element_type=F32)


def _split(a):
    hi = a.astype(BF16)
    lo = (a - hi.astype(F32)).astype(BF16)
    return hi, lo


def _dot3(a, b):
    ah, al = _split(a)
    bh, bl = _split(b)
    return _mm(ah, bh) + _mm(ah, bl) + _mm(al, bh)


def _layer_norm(x):
    mu = jnp.mean(x, axis=-1, keepdims=True)
    xc = x - mu
    var = jnp.mean(xc * xc, axis=-1, keepdims=True)
    return xc * lax.rsqrt(var + LN_EPS)


def _sigmoid(a):
    return 0.5 * jnp.tanh(0.5 * a) + 0.5


def _silu(a):
    return a * _sigmoid(a)


@functools.lru_cache(maxsize=None)
def _dft_consts(seq):
    n = 2 * seq
    na_full = n // NB
    na_half = na_full // 2
    ka = np.arange(na_full)[:, None]
    na = np.arange(na_half)[None, :]
    ang = 2.0 * np.pi * ((ka * na) % na_full) / na_full
    c, s = np.cos(ang), np.sin(ang)
    f1 = np.block([[c, s], [-s, c]])
    f1r = np.concatenate([c, -s], axis=0)
    f4 = np.block([[c.T, -s.T], [s.T, c.T]]) / n
    kav = np.arange(na_full)[:, None, None]
    kb = np.arange(NB)[None, :, None]
    nb = np.arange(NB)[None, None, :]
    th = 2.0 * np.pi * (((nb * kav) % n) / n + ((nb * kb) % NB) / NB)
    cg, sg = np.cos(th), np.sin(th)
    g = np.concatenate([np.concatenate([cg, sg], axis=2), np.concatenate([-sg, cg], axis=2)], axis=1)
    cgt, sgt = cg.transpose(0, 2, 1), sg.transpose(0, 2, 1)
    gi = np.concatenate([np.concatenate([cgt, -sgt], axis=2), np.concatenate([sgt, cgt], axis=2)], axis=1)
    as_bf16 = lambda a: jnp.asarray(a, dtype=F32).astype(BF16)
    return as_bf16(f1), as_bf16(f1r), as_bf16(f4), as_bf16(g), as_bf16(gi)


@functools.lru_cache(maxsize=None)
def _filter_consts(seq, channels):
    t = np.linspace(0.0, 1.0, seq)[:, None]
    bands = (FILTER_EMB - 1) // 2
    w = 2.0 * np.pi * np.arange(seq)[:, None] / seq
    f = np.linspace(1e-4, bands - 1, bands)[None, :]
    pos = np.concatenate([t, np.cos(f * w), -np.sin(f * w)], axis=-1)
    pos = np.pad(pos, ((0, 0), (0, PAD_EMB - FILTER_EMB)))
    deltas = np.linspace(math.log(DECAY_TARGET) / SLOW_DECAY_PCT, math.log(DECAY_TARGET) / FAST_DECAY_PCT, channels)
    decay = np.exp(-t * np.abs(deltas)[None, :])
    return jnp.asarray(pos, dtype=F32), jnp.asarray(decay, dtype=F32)


@functools.lru_cache(maxsize=None)
def _attn_bias(n_heads, dilation, radius):
    tq, tk = 2 * radius, 4 * radius
    slopes = 2.0 ** (-8.0 * (np.arange(n_heads) + 1) / n_heads)
    i = np.arange(tq)[:, None]
    j = np.arange(tk)[None, :]
    cases = []
    for off in (0, radius, 2 * radius):
        rel = np.abs(j - off - i)
        per_head = [np.where(rel <= radius, -slopes[h] * rel * dilation, NEG_BIG) for h in range(n_heads)]
        cases.append(np.stack([np.concatenate(per_head[2 * p:2 * p + 2], axis=0) for p in range(n_heads // 2)]))
    return jnp.asarray(np.stack(cases), dtype=F32)


def _ada_kernel(c_ref, w_ref, b_ref, o_ref):
    o_ref[...] = _dot3(c_ref[...], w_ref[...]) + b_ref[...]


def _ada(c, w_ada, b_ada):
    depth, d, n3 = w_ada.shape
    bsz = c.shape[0]
    tn = d
    return pl.pallas_call(
        _ada_kernel,
        grid=(depth, n3 // tn),
        in_specs=[
            pl.BlockSpec((bsz, d), lambda l, j: (0, 0)),
            pl.BlockSpec((None, d, tn), lambda l, j: (l, 0, j)),
            pl.BlockSpec((None, 1, tn), lambda l, j: (l, 0, j)),
        ],
        out_specs=pl.BlockSpec((None, bsz, tn), lambda l, j: (l, 0, j)),
        out_shape=jax.ShapeDtypeStruct((depth, bsz, n3), F32),
        name="ada_mod",
    )(c, w_ada, b_ada.reshape(depth, 1, n3))


def _inproj_kernel(x_ref, mod_ref, w_ref, b_ref, o_ref, h_ref):
    @pl.when(pl.program_id(2) == 0)
    def _():
        hn = _layer_norm(x_ref[...])
        shift, scale = mod_ref[0:1, :], mod_ref[1:2, :]
        h_ref[...] = (hn * (1.0 + scale) + shift).astype(BF16)

    o_ref[...] = (_mm(h_ref[...], w_ref[...]) + b_ref[...]).astype(o_ref.dtype)


def _inproj(x, mod, w, b, layer):
    bsz, seq, d = x.shape
    n = w.shape[-1]
    tm = ROW_TILE
    col = lambda rows: pl.BlockSpec((None, rows, COL_TILE), lambda bi, i, j: (layer, 0, j))
    return pl.pallas_call(
        _inproj_kernel,
        grid=(bsz, seq // tm, n // COL_TILE),
        in_specs=[
            pl.BlockSpec((None, tm, d), lambda bi, i, j: (bi, i, 0)),
            pl.BlockSpec((None, 3, d), lambda bi, i, j: (bi, 0, 0)),
            col(d), col(1),
        ],
        out_specs=pl.BlockSpec((None, tm, COL_TILE), lambda bi, i, j: (bi, i, j)),
        out_shape=jax.ShapeDtypeStruct((bsz, seq, n), BF16),
        scratch_shapes=[pltpu.VMEM((tm, d), BF16)],
        compiler_params=pltpu.CompilerParams(dimension_semantics=("parallel", "parallel", "arbitrary")),
        name="inproj",
    )(x, mod, w, b)


def _attn_kernel(q_ref, k_ref, v_ref, bias_ref, o_ref, lse_ref, *scratch, dilation, seq, radius):
    tq, tk = 2 * radius, 4 * radius
    d = dilation
    length = seq // d
    nblk = length // tq
    first_head = lax.broadcasted_iota(jnp.int32, (tq, LANES), 1) < HEAD_DIM
    pre = max(d // ATTN_STRIDE, 1)
    inner, sub = d // pre, seq // pre
    if d > 1:
        qf, kf, vf, of, lf = scratch[:5]
        for ref, buf in ((q_ref, qf), (k_ref, kf), (v_ref, vf)):
            if pre == 1:
                buf[...] = ref[...].astype(F32)
            else:
                tmp = scratch[5]
                tmp[...] = ref[...].astype(F32)
                for r1 in range(pre):
                    buf[pl.ds(r1 * sub, sub), :] = tmp[pl.ds(r1, sub, stride=pre), :]

    def rows(r, start, n):
        if d == 1:
            return pl.ds(pl.multiple_of(start, radius), n)
        r1, r2 = r & (pre - 1), lax.shift_right_logical(r, pre.bit_length() - 1)
        return pl.ds(r1 * sub + r2 + inner * start, n, stride=inner)

    def block(i, carry):
        r, jb = lax.shift_right_logical(i, nblk.bit_length() - 1), i & (nblk - 1)
        q0 = jb * tq
        ks = jnp.clip(q0 - radius, 0, length - tk)
        case = jnp.where(jb == 0, 0, jnp.where(jb == nblk - 1, 2, 1))
        if d == 1:
            q2, k2, v2 = q_ref[rows(r, q0, tq), :], k_ref[rows(r, ks, tk), :], v_ref[rows(r, ks, tk), :]
        else:
            q2, k2, v2 = (ref[rows(r, st, n), :].astype(BF16) for ref, st, n in ((qf, q0, tq), (kf, ks, tk), (vf, ks, tk)))
        q2 = q2 * (HEAD_DIM ** -0.5)
        zero = jnp.zeros_like(q2)
        qs = jnp.concatenate([jnp.where(first_head, q2, zero), jnp.where(first_head, zero, q2)], axis=0)
        s = lax.dot_general(qs, k2, (((1,), (1,)), ((), ())), preferred_element_type=F32) + bias_ref[case]
        m = jnp.max(s, axis=1, keepdims=True)
        e = jnp.exp(s - m).astype(BF16)
        res = _mm(e, jnp.concatenate([v2, jnp.ones_like(v2)], axis=1))
        den = jnp.where(first_head, res[:tq, LANES:], res[tq:, LANES:])
        o = jnp.where(first_head, res[:tq, :LANES], res[tq:, :LANES]) / den
        lse = jnp.where(first_head, m[:tq], m[tq:]) + jnp.log(den)
        if d == 1:
            o_ref[rows(r, q0, tq), :] = o.astype(o_ref.dtype)
            lse_ref[rows(r, q0, tq), :] = lse
        else:
            of[rows(r, q0, tq), :] = o
            lf[rows(r, q0, tq), :] = lse
        return carry

    lax.fori_loop(0, d * nblk, block, 0, unroll=ATTN_UNROLL)
    if d > 1 and pre == 1:
        o_ref[...] = of[...].astype(o_ref.dtype)
        lse_ref[...] = lf[...]
    elif d > 1:
        tmp = scratch[5]
        for r1 in range(pre):
            tmp[pl.ds(r1, sub, stride=pre), :] = of[pl.ds(r1 * sub, sub), :]
            lse_ref[pl.ds(r1, sub, stride=pre), :] = lf[pl.ds(r1 * sub, sub), :]
        o_ref[...] = tmp[...].astype(o_ref.dtype)


def _attention(p, qkv_cols, dilation, radius, n_heads):
    bsz, seq, _ = p.shape
    da = n_heads * HEAD_DIM
    n_pairs = da // LANES
    bias = _attn_bias(n_heads, dilation, radius)
    kern = functools.partial(_attn_kernel, dilation=dilation, seq=seq, radius=radius)
    col = lambda c0: (lambda bi, pr: (bi, 0, c0 // LANES + pr))
    return pl.pallas_call(
        kern,
        grid=(bsz, n_pairs),
        in_specs=[pl.BlockSpec((None, seq, LANES), col(c0)) for c0 in qkv_cols]
        + [pl.BlockSpec((3, None) + bias.shape[2:], lambda bi, pr: (0, pr, 0, 0))],
        out_specs=[pl.BlockSpec((None, seq, LANES), lambda bi, pr: (bi, 0, pr))] * 2,
        out_shape=[jax.ShapeDtypeStruct((bsz, seq, da), BF16), jax.ShapeDtypeStruct((bsz, seq, da), F32)],
        scratch_shapes=[pltpu.VMEM((seq, LANES), F32)] * (0 if dilation == 1 else 5 if dilation <= ATTN_STRIDE else 6),
        name=f"attn_d{dilation}",
    )(p, p, p, bias)


def _filter_kernel(pos_ref, w1_ref, b1_ref, w2_ref, b2_ref, w3_ref, b3_ref, w4_ref, fr_ref, dec_ref, o_ref):
    fr = fr_ref[...]
    h = jnp.sin(fr * (_dot3(pos_ref[...], w1_ref[...]) + b1_ref[...]))
    h = jnp.sin(fr * (_dot3(h, w2_ref[...]) + b2_ref[...]))
    h = jnp.sin(fr * (_dot3(h, w3_ref[...]) + b3_ref[...]))
    f = _dot3(h, w4_ref[...])
    dec = dec_ref[...]
    c = dec.shape[1]
    row = pl.program_id(0) * pos_ref.shape[0] + lax.broadcasted_iota(jnp.int32, (pos_ref.shape[0], 1), 0)
    for o in range(HYENA_ORDER):
        fwd = f[:, (2 * o) * c:(2 * o + 1) * c] * dec
        bwd = jnp.where(row == 0, 0.0, f[:, (2 * o + 1) * c:(2 * o + 2) * c] * dec)
        o_ref[:, (2 * o) * c:(2 * o + 1) * c] = (fwd + bwd).astype(o_ref.dtype)
        o_ref[:, (2 * o + 1) * c:(2 * o + 2) * c] = (bwd - fwd).astype(o_ref.dtype)


def _filters(seq, c, w1, b1, w2, b2, w3, b3, w4, freq):
    pos, decay = _filter_consts(seq, c)
    tm = 512
    w1p = jnp.pad(w1, ((0, PAD_EMB - FILTER_EMB), (0, 0)))
    full = lambda a: pl.BlockSpec(a.shape, lambda i: (0,) * a.ndim)
    args = (w1p, b1.reshape(1, -1), w2, b2.reshape(1, -1), w3, b3.reshape(1, -1), w4, freq.reshape(1, -1))
    return pl.pallas_call(
        _filter_kernel,
        grid=(seq // tm,),
        in_specs=[pl.BlockSpec((tm, PAD_EMB), lambda i: (i, 0))] + [full(a) for a in args]
        + [pl.BlockSpec((tm, c), lambda i: (i, 0))],
        out_specs=pl.BlockSpec((tm, 2 * HYENA_ORDER * c), lambda i: (i, 0)),
        out_shape=jax.ShapeDtypeStruct((seq, 2 * HYENA_ORDER * c), BF16),
        name="hyena_filters",
    )(pos, *args, decay)


def _filter_spectrum_kernel(e_ref, d_ref, g_ref, f1r_ref, h_ref, v_buf, s_buf, *, seq):
    t = pl.program_id(2)
    tile_rows, n_ka = e_ref.shape[0], g_ref.shape[0]
    nt = seq // tile_rows
    na_half = seq // NB
    na_full = 2 * na_half
    groups = tile_rows // NB

    @pl.when(t < nt)
    def _():
        for s, ref in enumerate((e_ref, d_ref)):
            for j in range(groups):
                rows = pl.ds(pl.multiple_of((t * groups + j) * PITCH_T, SUBLANES), NB)
                v_buf[s, rows, :] = ref[j * NB:(j + 1) * NB, :].astype(F32)

    @pl.when(t == nt - 1)
    def _():
        def body(nb, carry):
            rhs = jnp.concatenate([v_buf[s, pl.ds(nb, na_half, stride=PITCH_T), :] for s in range(2)], axis=1)
            y = _mm(f1r_ref[...], rhs.astype(BF16))
            for ri in range(2):
                for s in range(2):
                    s_buf[s, pl.ds(ri * NB + nb, na_full, stride=PITCH_F), :] = (
                        y[ri * na_full:(ri + 1) * na_full, s * LANES:(s + 1) * LANES])
            return carry
        lax.fori_loop(0, NB, body, 0, unroll=HY_UNROLL)

    @pl.when(t >= nt)
    def _():
        def body(l, carry):
            rows = pl.ds(pl.multiple_of(((t - nt) * n_ka + l) * PITCH_F, SUBLANES), 2 * NB)
            rhs = jnp.concatenate([s_buf[s, rows, :] for s in range(2)], axis=1)
            x = _mm(g_ref[l], rhs.astype(BF16))
            h_ref[l, 0:NB, :] = x[:NB, :LANES].astype(h_ref.dtype)
            h_ref[l, NB:2 * NB, :] = (-x[NB:, LANES:]).astype(h_ref.dtype)
            return carry
        lax.fori_loop(0, n_ka, body, 0, unroll=HY_UNROLL)


def _filter_spectra(filt, seq, c):
    _, f1r, _, g, _ = _dft_consts(seq)
    na_full = 2 * seq // NB
    nt, nk = seq // FS_ROWS, na_full // FS_KA
    slabs = c // LANES
    tile_of = lambda t: jnp.minimum(t, nt - 1)
    freq_of = lambda t: jnp.clip(t - nt, 0, nk - 1)
    return pl.pallas_call(
        functools.partial(_filter_spectrum_kernel, seq=seq),
        grid=(HYENA_ORDER, slabs, nt + nk),
        in_specs=[pl.BlockSpec((FS_ROWS, LANES), lambda o, s, t: (tile_of(t), (2 * o) * slabs + s)),
                  pl.BlockSpec((FS_ROWS, LANES), lambda o, s, t: (tile_of(t), (2 * o + 1) * slabs + s)),
                  pl.BlockSpec((FS_KA, 2 * NB, 2 * NB), lambda o, s, t: (freq_of(t), 0, 0)),
                  pl.BlockSpec(f1r.shape, lambda o, s, t: (0, 0))],
        out_specs=pl.BlockSpec((None, FS_KA, 2 * NB, LANES), lambda o, s, t: (o, freq_of(t), 0, s)),
        out_shape=jax.ShapeDtypeStruct((HYENA_ORDER, na_full, 2 * NB, c), BF16),
        scratch_shapes=[pltpu.VMEM((2, (na_full // 2) * PITCH_T, LANES), F32),
                        pltpu.VMEM((2, na_full * PITCH_F, LANES), F32)],
        compiler_params=pltpu.CompilerParams(dimension_semantics=("parallel", "parallel", "arbitrary")),
        name="hyena_filter_spectrum",
    )(filt, filt, g, f1r)


def _hyena_schedule(seq):
    nt, nk = seq // HY_ROWS, (2 * seq // NB) // HY_KA
    return {"spec0": nt, "mid": nt + nk, "spec1": 2 * nt + nk, "last": 2 * nt + 2 * nk, "total": 3 * nt + 2 * nk,
            "nt": nt, "nk": nk}


def _hyena_kernel(p_ref, prev_ref, next_ref, gh_ref, cw_ref, cb_ref, fb_ref, hf_ref, g_ref, gi_ref, f1_ref, f4_ref,
                  o_ref, v_buf, y_buf, s_buf, *, seq):
    sch = _hyena_schedule(seq)
    t = pl.program_id(2)
    n_slab = HY_LANES // LANES
    na_half = seq // NB
    na_full, pair_rows = 2 * na_half, 2 * na_half
    groups = HY_ROWS // NB
    lanes = lambda s: slice(s * LANES, (s + 1) * LANES)
    row = lax.broadcasted_iota(jnp.int32, (HY_ROWS, 1), 0)

    def short_conv(tile, b):
        a = p_ref[b].astype(F32)
        top = jnp.where(tile > 0, prev_ref[b, SUBLANES - 1:SUBLANES, :].astype(F32), 0.0)
        bot = jnp.where(tile < sch["nt"] - 1, next_ref[b, 0:1, :].astype(F32), 0.0)
        up = jnp.where(row == 0, top, pltpu.roll(a, 1, 0))
        dn = jnp.where(row == HY_ROWS - 1, bot, pltpu.roll(a, HY_ROWS - 1, 0))
        return cw_ref[0:1, :] * up + cw_ref[1:2, :] * a + cw_ref[2:3, :] * dn + cb_ref[...]

    def time_rows(b, tile, j):
        return pl.ds(pl.multiple_of((b * na_half + tile * groups + j) * PITCH_T, SUBLANES), NB)

    def forward_stage1():
        def body(nb, carry):
            rhs = jnp.concatenate([v_buf[s, pl.ds(nb, pair_rows, stride=PITCH_T), :] for s in range(n_slab)], axis=1)
            y = _mm(f1_ref[...], rhs.astype(BF16))
            for ri in range(2):
                for s in range(n_slab):
                    s_buf[s, pl.ds(ri * NB + nb, na_full, stride=PITCH_F), :] = y[ri * na_full:(ri + 1) * na_full, lanes(s)]
            return carry
        lax.fori_loop(0, NB, body, 0, unroll=HY_UNROLL)

    def spectral(step):
        def body(l, carry):
            rows = pl.ds(pl.multiple_of((step * HY_KA + l) * PITCH_F, SUBLANES), 2 * NB)
            rhs = jnp.concatenate([s_buf[s, rows, :] for s in range(n_slab)], axis=1)
            x = _mm(g_ref[l], rhs.astype(BF16))
            xr, xi = x[:NB], x[NB:]
            hr, hi = hf_ref[l, 0:NB, :].astype(F32), hf_ref[l, NB:2 * NB, :].astype(F32)
            prod = jnp.concatenate([xr * hr - xi * hi, xr * hi + xi * hr], axis=0)
            tt = _mm(gi_ref[l], prod.astype(BF16))
            for s in range(n_slab):
                s_buf[s, rows, :] = tt[:, lanes(s)]
            return carry
        lax.fori_loop(0, HY_KA, body, 0, unroll=HY_UNROLL)

    def inverse_stage():
        def body(nb, carry):
            parts = [jnp.concatenate([s_buf[s, pl.ds(ri * NB + nb, na_full, stride=PITCH_F), :] for s in range(n_slab)],
                                     axis=1) for ri in range(2)]
            y = _mm(f4_ref[...], jnp.concatenate(parts, axis=0).astype(BF16))
            for s in range(n_slab):
                y_buf[s, pl.ds(nb, pair_rows, stride=PITCH_T), :] = y[:, lanes(s)]
            return carry
        lax.fori_loop(0, NB, body, 0, unroll=HY_UNROLL)

    @pl.when(t < sch["spec0"])
    def _():
        for b in range(2):
            v = short_conv(t, b)
            for j in range(groups):
                for s in range(n_slab):
                    v_buf[s, time_rows(b, t, j), :] = v[j * NB:(j + 1) * NB, lanes(s)]

    @pl.when((t >= sch["mid"]) & (t < sch["spec1"]))
    def _():
        tile = t - sch["mid"]
        for b in range(2):
            x1 = short_conv(tile, b)
            for j in range(groups):
                for s in range(n_slab):
                    rows = time_rows(b, tile, j)
                    z = x1[j * NB:(j + 1) * NB, lanes(s)] * (y_buf[s, rows, :] + fb_ref[:, lanes(s)] * v_buf[s, rows, :])
                    v_buf[s, rows, :] = z

    @pl.when((t == sch["spec0"] - 1) | (t == sch["spec1"] - 1))
    def _():
        forward_stage1()

    @pl.when((t >= sch["spec0"]) & (t < sch["mid"]))
    def _():
        spectral(t - sch["spec0"])

    @pl.when((t >= sch["spec1"]) & (t < sch["last"]))
    def _():
        spectral(t - sch["spec1"])

    @pl.when((t == sch["mid"] - 1) | (t == sch["last"] - 1))
    def _():
        inverse_stage()

    @pl.when(t >= sch["last"])
    def _():
        tile = t - sch["last"]
        for b in range(2):
            x2 = short_conv(tile, b) * _silu(gh_ref[b].astype(F32))
            for j in range(groups):
                for s in range(n_slab):
                    rows = time_rows(b, tile, j)
                    z = x2[j * NB:(j + 1) * NB, lanes(s)] * (y_buf[s, rows, :] + fb_ref[:, lanes(s)] * v_buf[s, rows, :])
                    o_ref[b, j * NB:(j + 1) * NB, lanes(s)] = z.astype(o_ref.dtype)


def _hyena(p, col_hy, col_gate, conv_w, conv_b, filt_bias, hf, c):
    bsz, seq, ncols = p.shape
    f1, _, f4, g, gi = _dft_consts(seq)
    sch = _hyena_schedule(seq)
    nt, nk = sch["nt"], sch["nk"]
    n_pairs, na_full = bsz // 2, 2 * seq // NB
    groups_per_c = c // HY_LANES
    halo_per_tile = HY_ROWS // SUBLANES
    pv = p.reshape(n_pairs, 2, seq, ncols)

    def tile_of(t):
        return jnp.where(t < sch["mid"], jnp.minimum(t, nt - 1),
                         jnp.where(t < sch["last"], jnp.clip(t - sch["mid"], 0, nt - 1), jnp.clip(t - sch["last"], 0, nt - 1)))

    def col_of(t, lg):
        return jnp.where(t < sch["mid"], 0, jnp.where(t < sch["last"], 1, 2)) * groups_per_c + lg

    p_col = lambda t, lg: col_hy // HY_LANES + col_of(t, lg)

    def freq_of(t):
        return jnp.where(t < sch["spec1"], jnp.clip(t - sch["spec0"], 0, nk - 1), jnp.clip(t - sch["spec1"], 0, nk - 1))

    order_of = lambda t: jnp.where(t < sch["spec1"], 0, 1)
    last_tile = lambda t: jnp.clip(t - sch["last"], 0, nt - 1)
    const = lambda a: pl.BlockSpec(a.shape, lambda bp, lg, t: (0,) * a.ndim)
    width = 3 * c
    out = pl.pallas_call(
        functools.partial(_hyena_kernel, seq=seq),
        grid=(n_pairs, groups_per_c, sch["total"]),
        in_specs=[
            pl.BlockSpec((None, 2, HY_ROWS, HY_LANES), lambda bp, lg, t: (bp, 0, tile_of(t), p_col(t, lg))),
            pl.BlockSpec((None, 2, SUBLANES, HY_LANES),
                         lambda bp, lg, t: (bp, 0, jnp.maximum(tile_of(t) * halo_per_tile - 1, 0), p_col(t, lg))),
            pl.BlockSpec((None, 2, SUBLANES, HY_LANES),
                         lambda bp, lg, t: (bp, 0, jnp.minimum((tile_of(t) + 1) * halo_per_tile, seq // SUBLANES - 1),
                                            p_col(t, lg))),
            pl.BlockSpec((None, 2, HY_ROWS, HY_LANES), lambda bp, lg, t: (bp, 0, last_tile(t), col_gate // HY_LANES + lg)),
            pl.BlockSpec((3, HY_LANES), lambda bp, lg, t: (0, col_of(t, lg))),
            pl.BlockSpec((1, HY_LANES), lambda bp, lg, t: (0, col_of(t, lg))),
            pl.BlockSpec((None, 1, HY_LANES), lambda bp, lg, t: (order_of(t), 0, lg)),
            pl.BlockSpec((None, HY_KA, 2 * NB, HY_LANES), lambda bp, lg, t: (order_of(t), freq_of(t), 0, lg)),
            pl.BlockSpec((HY_KA, 2 * NB, 2 * NB), lambda bp, lg, t: (freq_of(t), 0, 0)),
            pl.BlockSpec((HY_KA, 2 * NB, 2 * NB), lambda bp, lg, t: (freq_of(t), 0, 0)),
            const(f1), const(f4),
        ],
        out_specs=pl.BlockSpec((None, 2, HY_ROWS, HY_LANES), lambda bp, lg, t: (bp, 0, last_tile(t), lg)),
        out_shape=jax.ShapeDtypeStruct((n_pairs, 2, seq, c), BF16),
        scratch_shapes=[
            pltpu.VMEM((HY_LANES // LANES, na_full * PITCH_T, LANES), F32),
            pltpu.VMEM((HY_LANES // LANES, na_full * PITCH_T, LANES), F32),
            pltpu.VMEM((HY_LANES // LANES, na_full * PITCH_F, LANES), F32),
        ],
        compiler_params=pltpu.CompilerParams(dimension_semantics=("parallel", "parallel", "arbitrary")),
        name="hyena_long_conv",
    )(pv, pv, pv, pv, conv_w, conv_b.reshape(1, width), filt_bias.reshape(HYENA_ORDER, 1, c), hf, g, gi, f1, f4)
    return out.reshape(bsz, seq, c)


def _out_kernel(o0_ref, o1_ref, o2_ref, l0_ref, l1_ref, l2_ref, ga_ref, yh_ref, ra_ref, rh_ref, x_ref, mod_ref,
                wpa_ref, wph_ref, wo_ref, bo_ref, lg_ref, lb_ref, out_ref, *, alpha):
    lses = [l0_ref[...], l1_ref[...], l2_ref[...]]
    m = jnp.maximum(jnp.maximum(lses[0], lses[1]), lses[2])
    es = [jnp.exp(l - m) for l in lses]
    o = es[0] * o0_ref[...].astype(F32) + es[1] * o1_ref[...].astype(F32) + es[2] * o2_ref[...].astype(F32)
    o = o / (es[0] + es[1] + es[2])
    ya = (o * _silu(ga_ref[...].astype(F32))).astype(BF16)
    merged = (_sigmoid(ra_ref[...].astype(F32)) * _mm(ya, wpa_ref[...])
              + _sigmoid(rh_ref[...].astype(F32)) * _mm(yh_ref[...], wph_ref[...]))
    out = _mm(merged.astype(BF16), wo_ref[...]) + bo_ref[...]
    res = alpha * x_ref[...] + mod_ref[2:3, :] * out
    out_ref[...] = _layer_norm(res) * lg_ref[...] + lb_ref[...]


def _out_proj(os_, lses, p, cols, yh, x, mod, wpa, wph, wo, bo, lg, lb, alpha):
    bsz, seq, d = x.shape
    da = wpa.shape[0]
    tm = OUT_TILE
    rows = lambda width, cb=0: pl.BlockSpec((None, tm, width), lambda bi, i: (bi, i, cb))
    const = lambda a: pl.BlockSpec(a.shape, lambda bi, i: (0,) * a.ndim)
    ga_col, ra_col, rh_col = cols
    return pl.pallas_call(
        functools.partial(_out_kernel, alpha=alpha),
        grid=(bsz, seq // tm),
        in_specs=[rows(da)] * 6
        + [rows(da, ga_col // da), rows(da), rows(d, ra_col // d), rows(d, rh_col // d), rows(d),
           pl.BlockSpec((None, 3, d), lambda bi, i: (bi, 0, 0)),
           const(wpa), const(wph), const(wo)]
        + [pl.BlockSpec((1, d), lambda bi, i: (0, 0))] * 3,
        out_specs=rows(d),
        out_shape=jax.ShapeDtypeStruct((bsz, seq, d), F32),
        name="merge_out_proj",
    )(*os_, *lses, p, yh, p, p, x, mod, wpa, wph, wo, bo.reshape(1, d), lg.reshape(1, d), lb.reshape(1, d))


def _layer(x, mod, layer, w_in_all, b_in_all, conv_w, conv_b, fw1, fb1, fw2, fb2, fw3, fb3, fw4, ffreq, fbias,
           w_proj_attn, w_proj_hyena, w_out, b_out, ln_g, ln_b, alpha):
    bsz, seq, d = x.shape
    da = w_proj_attn.shape[0]
    c = w_proj_hyena.shape[0]
    n_heads = da // HEAD_DIM

    col_ga = 3 * N_GROUPS * da
    col_hy = col_ga + da
    col_gh = col_hy + 3 * c
    col_ra = col_gh + c
    col_rh = col_ra + d

    p = _inproj(x, mod, w_in_all, b_in_all, layer)

    outs, lses = [], []
    for g, (window, dilation) in enumerate(DILATED_GROUPS):
        qkv_cols = [(i * N_GROUPS + g) * da for i in range(3)]
        o_g, lse_g = _attention(p, qkv_cols, dilation, window // (2 * dilation), n_heads)
        outs.append(o_g)
        lses.append(lse_g)

    filt = _filters(seq, c, fw1, fb1, fw2, fb2, fw3, fb3, fw4, ffreq)
    hf = _filter_spectra(filt, seq, c)
    yh = _hyena(p, col_hy, col_gh, conv_w, conv_b, fbias, hf, c)

    return _out_proj(outs, lses, p, (col_ga, col_ra, col_rh), yh, x, mod,
                     w_proj_attn.astype(BF16), w_proj_hyena.astype(BF16), w_out.astype(BF16),
                     b_out, ln_g, ln_b, alpha)


def kernel(x, c, w_ada, b_ada, w_in, b_in, conv_w, conv_b, filt_w1, filt_b1, filt_w2, filt_b2, filt_w3, filt_b3,
           filt_w4, filt_freq, filt_bias, w_proj_attn, w_proj_hyena, w_out, b_out, ln_g, ln_b):
    depth = w_ada.shape[0]
    alpha = (2 * depth) ** 0.25
    d = x.shape[-1]
    mods = _ada(c, w_ada, b_ada).reshape(depth, c.shape[0], 3, d)
    w_in_bf16, b_in_rows = w_in.astype(BF16), b_in.reshape(depth, 1, -1)
    for l in range(depth):
        x = _layer(x, mods[l], l, w_in_bf16, b_in_rows, conv_w[l], conv_b[l], filt_w1[l], filt_b1[l], filt_w2[l],
                   filt_b2[l], filt_w3[l], filt_b3[l], filt_w4[l], filt_freq[l], filt_bias[l],
                   w_proj_attn[l], w_proj_hyena[l], w_out[l], b_out[l], ln_g[l], ln_b[l], alpha)
    return x
```
